```python
import math
import jax, jax.numpy as jnp
from jax import lax
import numpy as np

D_MODEL = 1024
BATCH = 4
SEQ = 4096
DEPTH = 4
DEC_BATCH = 128
DEC_SEQ = 8
PAST_LEN = 2048
PAGE_SIZE = 128

HD_A = 64
H_A = D_MODEL // 128
D_ATTN = H_A * HD_A
SSM_CG = 16
SSM_GROUPS = D_MODEL // 64
SSM_P = 64
D_SSM = SSM_GROUPS * SSM_CG
HD_M = 64
H_M = D_MODEL // 256
D_MLSTM = H_M * HD_M
D_MIX = D_ATTN + D_SSM + D_MLSTM
IN_SIZES = (D_ATTN, D_ATTN, D_ATTN, D_SSM, D_MLSTM, D_MLSTM, D_MLSTM, D_MLSTM, H_M, H_M)
D_IN = 3 * D_ATTN + D_SSM + 4 * D_MLSTM + 2 * H_M
D_FF = ((8 * D_MODEL + 3 * 256 - 1) // (3 * 256)) * 256
ALPHA = (2 * DEPTH) ** 0.25
BETA = (8 * DEPTH) ** -0.25
Q_BLOCK = 128
M_CHUNK = 64
LN_EPS = 1e-5
RMS_EPS = 1e-6
SB_LOGIT_OFFSET = 5.0
SB_QK_SCALE = 0.3

kernel_name = 'hybrid_sb_s5_mlstm_step'


def _layer_norm(x, g, b):
    xf = x.astype(jnp.float32)
    mu = jnp.mean(xf, axis=-1, keepdims=True)
    var = jnp.mean(jnp.square(xf - mu), axis=-1, keepdims=True)
    return ((xf - mu) * lax.rsqrt(var + LN_EPS) * g + b).astype(x.dtype)


def _rms(y):
    yf = y.astype(jnp.float32)
    return yf * lax.rsqrt(jnp.mean(yf * yf, axis=-1, keepdims=True) + RMS_EPS)


def _sb_block(q, k, v, q_pos, k_pos):
    z = jnp.einsum('bqhd,bkhd->bhqk', q, k, preferred_element_type=jnp.float32) * (HD_A ** -0.5)
    mask = k_pos[None, :] < q_pos[:, None]
    log_skip = jnp.where(mask, jax.nn.log_sigmoid(-z), 0.0)
    log_after = lax.cumsum(log_skip, axis=3, reverse=True) - log_skip
    w = jnp.where(mask, jnp.exp(jax.nn.log_sigmoid(z) + log_after), 0.0)
    return jnp.einsum('bhqk,bkhd->bqhd', w.astype(v.dtype), v)


def _sb_attention(q, k, v, q_pos, k_pos):
    tq = q.shape[1]
    if tq <= Q_BLOCK or tq % Q_BLOCK:
        return _sb_block(q, k, v, q_pos, k_pos)
    nb = tq // Q_BLOCK
    qb = q.reshape(q.shape[0], nb, Q_BLOCK, *q.shape[2:]).swapaxes(0, 1)
    pb = q_pos.reshape(nb, Q_BLOCK)
    out = lax.map(lambda a: _sb_block(a[0], k, v, a[1], k_pos), (qb, pb))
    return out.swapaxes(0, 1).reshape(q.shape)


def _complex_affine(e1, e2):
    a1r, a1i, b1r, b1i = e1
    a2r, a2i, b2r, b2i = e2
    return (a2r * a1r - a2i * a1i, a2r * a1i + a2i * a1r,
            a2r * b1r - a2i * b1i + b2r, a2r * b1i + a2i * b1r + b2i)


def _s5(u, h0_re, h0_im, lam_re, lam_im, log_dt, b_re, b_im, c_re, c_im, d_skip):
    f32 = jnp.float32
    u = u.astype(f32)
    lam_re, lam_im, b_re, b_im, c_re, c_im, d_skip = (
        a.astype(f32) for a in (lam_re, lam_im, b_re, b_im, c_re, c_im, d_skip))
    dt = jnp.exp(log_dt.astype(f32))[:, None]
    mag = jnp.exp(lam_re * dt)
    a_re = mag * jnp.cos(lam_im * dt)
    a_im = mag * jnp.sin(lam_im * dt)
    den = lam_re * lam_re + lam_im * lam_im
    z_re = ((a_re - 1.0) * lam_re + a_im * lam_im) / den
    z_im = (a_im * lam_re - (a_re - 1.0) * lam_im) / den
    bb_re = z_re[..., None] * b_re - z_im[..., None] * b_im
    bb_im = z_re[..., None] * b_im + z_im[..., None] * b_re
    bu_re = jnp.einsum('btgc,gpc->btgp', u, bb_re)
    bu_im = jnp.einsum('btgc,gpc->btgp', u, bb_im)
    h0_re = h0_re.astype(f32)
    h0_im = h0_im.astype(f32)
    bu_re = bu_re.at[:, 0].add(a_re * h0_re - a_im * h0_im)
    bu_im = bu_im.at[:, 0].add(a_re * h0_im + a_im * h0_re)
    ar = jnp.broadcast_to(a_re, bu_re.shape)
    ai = jnp.broadcast_to(a_im, bu_im.shape)
    _, _, h_re, h_im = lax.associative_scan(_complex_affine, (ar, ai, bu_re, bu_im), axis=1)
    y = (jnp.einsum('gcp,btgp->btgc', c_re, h_re) - jnp.einsum('gcp,btgp->btgc', c_im, h_im)
         + d_skip * u)
    return y.reshape(y.shape[0], y.shape[1], -1), h_re[:, -1], h_im[:, -1]


def _mlstm(q, k, v, i_pre, f_pre, c0, n0, m0):
    bsz, t, nh, d = q.shape
    L = math.gcd(t, M_CHUNK)
    nc = t // L
    f32 = jnp.float32

    def chunks(a):
        a = a.astype(f32).reshape(bsz, nc, L, nh, *a.shape[3:])
        return jnp.moveaxis(a, (1, 3), (0, 2))

    xs = (chunks(q), chunks(k) * (d ** -0.5), chunks(v), chunks(i_pre),
          chunks(jax.nn.log_sigmoid(f_pre.astype(f32))))
    tril = jnp.tril(jnp.ones((L, L), dtype=bool))

    def step(carry, xc):
        c, n, m = carry
        qc, kc, vc, ic, lfc = xc
        b = jnp.cumsum(lfc, axis=-1)
        log_d = jnp.where(tril, b[..., :, None] - b[..., None, :] + ic[..., None, :], -jnp.inf)
        log_inter = b + m[..., None]
        m_t = jnp.maximum(log_inter, jnp.max(log_d, axis=-1))
        s = jnp.einsum('bhtd,bhsd->bhts', qc, kc) * jnp.exp(log_d - m_t[..., None])
        w_inter = jnp.exp(log_inter - m_t)
        num = (jnp.einsum('bhts,bhsd->bhtd', s, vc)
               + w_inter[..., None] * jnp.einsum('bhvk,bhtk->bhtv', c, qc))
        den = jnp.sum(s, axis=-1) + w_inter * jnp.einsum('bhk,bhtk->bht', n, qc)
        h_t = num / jnp.maximum(jnp.abs(den), jnp.exp(-m_t))[..., None]
        m_new = m_t[..., -1]
        w_s = jnp.exp(b[..., -1:] - b + ic - m_new[..., None])
        decay = jnp.exp(b[..., -1] + m - m_new)
        c_new = decay[..., None, None] * c + jnp.einsum('bhs,bhsv,bhsk->bhvk', w_s, vc, kc)
        n_new = decay[..., None] * n + jnp.einsum('bhs,bhsk->bhk', w_s, kc)
        return (c_new, n_new, m_new), h_t

    (c1, n1, m1), h = lax.scan(step, (c0.astype(f32), n0.astype(f32), m0.astype(f32)), xs)
    h = jnp.moveaxis(h, (0, 2), (1, 3)).reshape(bsz, t, nh, d)
    return h, c1, n1, m1


def _trunk_layer(x, k_past, v_past, ssm_re0, ssm_im0, c0, n0, m0,
                 w_in, b_in, gate_b, mix_g, w_out, ln1_g, ln1_b,
                 lam_re, lam_im, log_dt, b_re, b_im, c_re, c_im, d_skip, w_glu, b_glu,
                 w_gate, w_up, w_down, ln2_g, ln2_b):
    bsz, t, _ = x.shape
    past = 0 if k_past is None else k_past.shape[1]
    proj = x @ w_in + b_in
    idx = np.cumsum(IN_SIZES)[:-1].tolist()
    qa, ka, va, us, qm, km, vm, om, im, fm = jnp.split(proj, idx, axis=-1)
    qa = qa.reshape(bsz, t, H_A, HD_A)
    ka = ka.reshape(bsz, t, H_A, HD_A)
    va = va.reshape(bsz, t, H_A, HD_A)
    q_pos = past + jnp.arange(t)
    k_pos = jnp.arange(past + t)
    k_all = ka if k_past is None else jnp.concatenate([k_past.astype(ka.dtype), ka], axis=1)
    v_all = va if v_past is None else jnp.concatenate([v_past.astype(va.dtype), va], axis=1)
    y_a = _sb_attention(qa, k_all, v_all, q_pos, k_pos)
    y_s, h_re, h_im = _s5(us.reshape(bsz, t, SSM_GROUPS, SSM_CG), ssm_re0, ssm_im0,
                          lam_re, lam_im, log_dt, b_re, b_im, c_re, c_im, d_skip)
    g = jax.nn.gelu(y_s)
    y_s = g * jax.nn.sigmoid(g @ w_glu.astype(jnp.float32) + b_glu.astype(jnp.float32))
    y_m, c1, n1, m1 = _mlstm(qm.reshape(bsz, t, H_M, HD_M), km.reshape(bsz, t, H_M, HD_M),
                             vm.reshape(bsz, t, H_M, HD_M), im + gate_b[:H_M], fm + gate_b[H_M:],
                             c0, n0, m0)
    y_m = y_m * jax.nn.sigmoid(om.astype(jnp.float32)).reshape(bsz, t, H_M, HD_M)
    mix = jnp.concatenate([_rms(y_a).reshape(bsz, t, D_ATTN), _rms(y_s),
                           _rms(y_m).reshape(bsz, t, D_MLSTM)], axis=-1) * mix_g
    x = _layer_norm(ALPHA * x + mix.astype(x.dtype) @ w_out, ln1_g, ln1_b)
    f = (jax.nn.silu(x @ w_gate) * (x @ w_up)) @ w_down
    x = _layer_norm(ALPHA * x + f, ln2_g, ln2_b)
    return x, ka, va, h_re, h_im, c1, n1, m1


def setup_inputs(seed: int = 0) -> dict:
    key = jax.random.key(seed)
    ks = iter(jax.random.split(key, 64))
    f32 = jnp.float32

    def nrm(shape, scale):
        return jax.random.normal(next(ks), shape, f32) * scale

    n_pages = PAST_LEN // PAGE_SIZE
    n_used = DEC_BATCH * n_pages
    n_pool = n_used + n_used // 4
    page_table = jax.random.permutation(next(ks), n_pool)[:n_used].reshape(DEC_BATCH, n_pages).astype(jnp.int32)
    f_bias = jnp.broadcast_to(jnp.linspace(3.0, 6.0, H_M, dtype=f32), (DEPTH, H_M)) + nrm((DEPTH, H_M), 0.1)
    i_bias = nrm((DEPTH, H_M), 0.1)
    sb_dir = nrm((DEPTH, H_A, HD_A), 1.0)
    sb_dir = sb_dir / jnp.linalg.norm(sb_dir, axis=-1, keepdims=True)
    sb_off = math.sqrt(SB_LOGIT_OFFSET * math.sqrt(HD_A))
    sb_flat = sb_dir.reshape(DEPTH, D_ATTN)
    col_scale = jnp.concatenate([jnp.full((2 * D_ATTN,), SB_QK_SCALE, f32),
                                 jnp.ones((D_IN - 2 * D_ATTN,), f32)])
    b_shift = jnp.concatenate([sb_off * sb_flat, -sb_off * sb_flat,
                               jnp.zeros((DEPTH, D_IN - 2 * D_ATTN), f32)], axis=-1)
    cache_k = (nrm((DEPTH, n_pool, PAGE_SIZE, H_A, HD_A), SB_QK_SCALE)
               - sb_off * sb_dir[:, None, None, :, :])
    return {
        'x_prompt': nrm((BATCH, SEQ, D_MODEL), 1.0),
        'x_sample': nrm((DEC_BATCH, DEC_SEQ, D_MODEL), 1.0),
        'cache_k': cache_k,
        'cache_v': nrm((DEPTH, n_pool, PAGE_SIZE, H_A, HD_A), 1.0),
        'page_table': page_table,
        'state_ssm_re': nrm((DEPTH, DEC_BATCH, SSM_GROUPS, SSM_P), 0.1),
        'state_ssm_im': nrm((DEPTH, DEC_BATCH, SSM_GROUPS, SSM_P), 0.1),
        'state_mlstm_c': nrm((DEPTH, DEC_BATCH, H_M, HD_M, HD_M), 0.1),
        'state_mlstm_n': nrm((DEPTH, DEC_BATCH, H_M, HD_M), 0.5),
        'state_mlstm_m': nrm((DEPTH, DEC_BATCH, H_M), 1.0),
        'w_in': nrm((DEPTH, D_MODEL, D_IN), D_MODEL ** -0.5) * col_scale,
        'b_in': nrm((DEPTH, D_IN), 0.01) + b_shift,
        'mlstm_gate_b': jnp.concatenate([i_bias, f_bias], axis=-1),
        'mix_g': 1.0 + nrm((DEPTH, D_MIX), 0.02),
        'w_out': nrm((DEPTH, D_MIX, D_MODEL), BETA * D_MIX ** -0.5),
        'ln1_g': 1.0 + nrm((DEPTH, D_MODEL), 0.02),
        'ln1_b': nrm((DEPTH, D_MODEL), 0.02),
        'ssm_lambda_re': -0.5 * jnp.exp(nrm((DEPTH, SSM_GROUPS, SSM_P), 0.05)),
        'ssm_lambda_im': jnp.pi * jnp.arange(SSM_P, dtype=f32) + nrm((DEPTH, SSM_GROUPS, SSM_P), 0.01),
        'ssm_log_dt': jax.random.uniform(next(ks), (DEPTH, SSM_GROUPS), f32, math.log(1e-3), math.log(1e-1)),
        'ssm_b_re': nrm((DEPTH, SSM_GROUPS, SSM_P, SSM_CG), (2 * SSM_CG) ** -0.5),
        'ssm_b_im': nrm((DEPTH, SSM_GROUPS, SSM_P, SSM_CG), (2 * SSM_CG) ** -0.5),
        'ssm_c_re': nrm((DEPTH, SSM_GROUPS, SSM_CG, SSM_P), (2 * SSM_P) ** -0.5),
        'ssm_c_im': nrm((DEPTH, SSM_GROUPS, SSM_CG, SSM_P), (2 * SSM_P) ** -0.5),
        'ssm_d': nrm((DEPTH, SSM_GROUPS, SSM_CG), 1.0),
        'ssm_w_glu': nrm((DEPTH, D_SSM, D_SSM), D_SSM ** -0.5),
        'ssm_b_glu': nrm((DEPTH, D_SSM), 0.01),
        'w_gate': nrm((DEPTH, D_MODEL, D_FF), D_MODEL ** -0.5),
        'w_up': nrm((DEPTH, D_MODEL, D_FF), D_MODEL ** -0.5),
        'w_down': nrm((DEPTH, D_FF, D_MODEL), BETA * D_FF ** -0.5),
        'ln2_g': 1.0 + nrm((DEPTH, D_MODEL), 0.02),
        'ln2_b': nrm((DEPTH, D_MODEL), 0.02),
    }


def reference(x_prompt, x_sample, cache_k, cache_v, page_table, state_ssm_re, state_ssm_im,
              state_mlstm_c, state_mlstm_n, state_mlstm_m, w_in, b_in, mlstm_gate_b, mix_g, w_out,
              ln1_g, ln1_b, ssm_lambda_re, ssm_lambda_im, ssm_log_dt, ssm_b_re, ssm_b_im,
              ssm_c_re, ssm_c_im, ssm_d, ssm_w_glu, ssm_b_glu, w_gate, w_up, w_down, ln2_g, ln2_b):
    f32 = jnp.float32
    bp, tp = x_prompt.shape[0], x_prompt.shape[1]
    bs = x_sample.shape[0]
    n_pages = page_table.shape[1]
    page = cache_k.shape[2]
    zero_ssm = jnp.zeros((bp, SSM_GROUPS, SSM_P), f32)
    zero_c = jnp.zeros((bp, H_M, HD_M, HD_M), f32)
    zero_n = jnp.zeros((bp, H_M, HD_M), f32)
    zero_m = jnp.zeros((bp, H_M), f32)
    y_p, y_s = x_prompt, x_sample
    st_p = [[] for _ in range(7)]
    st_s = [[] for _ in range(7)]
    for l in range(DEPTH):
        lw = (w_in[l], b_in[l], mlstm_gate_b[l], mix_g[l], w_out[l], ln1_g[l], ln1_b[l],
              ssm_lambda_re[l], ssm_lambda_im[l], ssm_log_dt[l], ssm_b_re[l], ssm_b_im[l],
              ssm_c_re[l], ssm_c_im[l], ssm_d[l], ssm_w_glu[l], ssm_b_glu[l],
              w_gate[l], w_up[l], w_down[l], ln2_g[l], ln2_b[l])
        y_p, *new_p = _trunk_layer(y_p, None, None, zero_ssm, zero_ssm, zero_c, zero_n, zero_m, *lw)
        k_past = cache_k[l][page_table].reshape(bs, n_pages * page, H_A, HD_A)
        v_past = cache_v[l][page_table].reshape(bs, n_pages * page, H_A, HD_A)
        y_s, *new_s = _trunk_layer(y_s, k_past, v_past, state_ssm_re[l], state_ssm_im[l],
                                   state_mlstm_c[l], state_mlstm_n[l], state_mlstm_m[l], *lw)
        for lst, a in zip(st_p, new_p):
            lst.append(a)
        for lst, a in zip(st_s, new_s):
            lst.append(a)
    k_prompt = jnp.stack(st_p[0]).reshape(DEPTH, bp, tp // page, page, H_A, HD_A)
    v_prompt = jnp.stack(st_p[1]).reshape(DEPTH, bp, tp // page, page, H_A, HD_A)
    ssm_re_prompt, ssm_im_prompt = jnp.stack(st_p[2]), jnp.stack(st_p[3])
    c_prompt, n_prompt, m_prompt = jnp.stack(st_p[4]), jnp.stack(st_p[5]), jnp.stack(st_p[6])
    k_sample, v_sample = jnp.stack(st_s[0]), jnp.stack(st_s[1])
    ssm_re_sample, ssm_im_sample = jnp.stack(st_s[2]), jnp.stack(st_s[3])
    c_sample, n_sample, m_sample = jnp.stack(st_s[4]), jnp.stack(st_s[5]), jnp.stack(st_s[6])
    return (y_p, y_s, k_prompt, v_prompt, k_sample, v_sample,
            ssm_re_prompt, ssm_im_prompt, ssm_re_sample, ssm_im_sample,
            c_prompt, n_prompt, m_prompt, c_sample, n_sample, m_sample)
```

```python
import functools
import math

import jax
import jax.numpy as jnp
from jax import lax
from jax.experimental import pallas as pl
from jax.experimental.pallas import tpu as pltpu

F32 = jnp.float32
BF16 = jnp.bfloat16

LN_EPS = 1e-5
RMS_EPS = 1e-6
V7X_VMEM_LIMIT_BYTES = 56 * 1024 * 1024
LANES = 128
SUBLANES = 8


def _dot(a, b):
    return jnp.dot(a, b, preferred_element_type=F32)


def _dot_nt(a, b):
    return lax.dot_general(a, b, (((1,), (1,)), ((), ())), preferred_element_type=F32)


def _dot_tn(a, b):
    return lax.dot_general(a, b, (((0,), (0,)), ((), ())), preferred_element_type=F32)


def _split2(x):
    hi = x.astype(BF16)
    lo = (x - hi.astype(F32)).astype(BF16)
    return hi, lo


def _split3(x):
    hi = x.astype(BF16)
    r = x - hi.astype(F32)
    mid = r.astype(BF16)
    lo = (r - mid.astype(F32)).astype(BF16)
    return hi, mid, lo


def _iota(shape, dim):
    return lax.broadcasted_iota(jnp.int32, shape, dim)


def _log2(n):
    k = int(math.log2(n))
    assert 1 << k == n, n
    return k


def _pick_tile(n, candidates):
    for c in candidates:
        if n % c == 0:
            return c
    raise ValueError(f"no tile for {n} in {candidates}")


def _params(*sem):
    return pltpu.CompilerParams(dimension_semantics=sem, vmem_limit_bytes=V7X_VMEM_LIMIT_BYTES)


def _const_spec(shape):
    nd = len(shape)
    return pl.BlockSpec(shape, lambda *_: (0,) * nd)


def _proj_kernel(x_ref, w_ref, b_ref, wg_ref, bg_ref,
                 q_ref, k_ref, v_ref, kb_ref, vb_ref, us_ref, qm_ref, km_ref, vm_ref, om_ref, g_ref,
                 *, d_attn, d_ssm, d_ml, q_scale):
    xb = x_ref[...].astype(BF16)

    def cols(c0, width):
        return _dot(xb, w_ref[:, c0:c0 + width]) + b_ref[:, c0:c0 + width]

    c = 0
    q_ref[...] = cols(c, d_attn) * q_scale
    c += d_attn
    k = cols(c, d_attn)
    k_ref[...] = k
    kb_ref[...] = k.astype(BF16)
    c += d_attn
    v = cols(c, d_attn)
    v_ref[...] = v
    vb_ref[...] = v.astype(BF16)
    c += d_attn
    us_ref[...] = cols(c, d_ssm)
    c += d_ssm
    for ref in (qm_ref, km_ref, vm_ref, om_ref):
        ref[...] = cols(c, d_ml)
        c += d_ml
    g_ref[...] = _dot(xb, wg_ref[...]) + bg_ref[...]


def _proj(x, w_main, b_main, w_gate, b_gate, *, d_attn, d_ssm, d_ml, q_scale):
    n, d = x.shape
    tm = _pick_tile(n, (512, 256, 128, 64))
    widths = (d_attn, d_attn, d_attn, d_attn, d_attn, d_ssm, d_ml, d_ml, d_ml, d_ml, LANES)
    dtypes = (F32, F32, F32, BF16, BF16, F32, F32, F32, F32, F32, F32)
    row_spec = lambda w: pl.BlockSpec((tm, w), lambda i: (i, 0))
    return pl.pallas_call(
        functools.partial(_proj_kernel, d_attn=d_attn, d_ssm=d_ssm, d_ml=d_ml, q_scale=q_scale),
        grid=(n // tm,),
        in_specs=[row_spec(d), _const_spec(w_main.shape), _const_spec(b_main.shape),
                  _const_spec(w_gate.shape), _const_spec(b_gate.shape)],
        out_specs=[row_spec(w) for w in widths],
        out_shape=[jax.ShapeDtypeStruct((n, w), dt) for w, dt in zip(widths, dtypes)],
        compiler_params=_params("arbitrary"),
        name="proj",
    )(x, w_main, b_main, w_gate, b_gate)


def _sb_tile(z, carry, u_tri, mask):
    soft = jnp.log1p(jnp.exp(-jnp.abs(z)))
    log_skip = -(jnp.maximum(z, 0.0) + soft)
    log_beta = jnp.minimum(z, 0.0) - soft
    if mask is not None:
        log_skip = jnp.where(mask, log_skip, 0.0)
    hi, lo = _split2(log_skip)
    after = _dot(hi, u_tri) + _dot(lo, u_tri)
    w = jnp.exp(log_beta + after + carry)
    if mask is not None:
        w = jnp.where(mask, w, 0.0)
    return w, carry + jnp.sum(log_skip, axis=-1, keepdims=True)


def _attn_prompt_kernel(q_ref, k_ref, v_ref, o_ref, *, tq, n_pairs, hd):
    qi = pl.program_id(1)
    pw = 2 * hd
    row = _iota((2 * tq, tq), 0)
    col = _iota((2 * tq, tq), 1)
    diag_mask = col < jnp.where(row >= tq, row - tq, row)
    u_tri = (_iota((tq, tq), 0) > _iota((tq, tq), 1)).astype(BF16)
    lo_half = _iota((tq, pw), 1) < hd
    diag0 = pl.multiple_of(qi * tq, tq)

    for p in range(n_pairs):
        c0 = p * pw
        qp = q_ref[:, c0:c0 + pw]
        q2 = jnp.concatenate([jnp.where(lo_half, qp, 0.0), jnp.where(lo_half, 0.0, qp)],
                             axis=0).astype(BF16)

        def tile(start, carry, acc, mask):
            kblk = k_ref[pl.ds(start, tq), c0:c0 + pw]
            vblk = v_ref[pl.ds(start, tq), c0:c0 + pw]
            w, carry = _sb_tile(_dot_nt(q2, kblk), carry, u_tri, mask)
            wb = w.astype(BF16)
            acc = acc + jnp.where(lo_half, _dot(wb[:tq], vblk), _dot(wb[tq:], vblk))
            return carry, acc

        carry, acc = tile(diag0, jnp.zeros((2 * tq, 1), F32), jnp.zeros((tq, pw), F32), diag_mask)

        def body(i, ca):
            start = pl.multiple_of((qi - 1 - i) * tq, tq)
            return tile(start, ca[0], ca[1], None)

        carry, acc = lax.fori_loop(0, qi, body, (carry, acc))

        sq = acc * acc
        ms_lo = jnp.sum(jnp.where(lo_half, sq, 0.0), axis=-1, keepdims=True) * (1.0 / hd)
        ms_hi = jnp.sum(jnp.where(lo_half, 0.0, sq), axis=-1, keepdims=True) * (1.0 / hd)
        scale = jnp.where(lo_half, lax.rsqrt(ms_lo + RMS_EPS), lax.rsqrt(ms_hi + RMS_EPS))
        o_ref[:, c0:c0 + pw] = acc * scale


def _attn_prompt(q, kb, vb, *, bsz, t, hd):
    n, d_attn = q.shape
    assert n == bsz * t and d_attn % (2 * hd) == 0
    tq = _pick_tile(t, (128, 64))
    nq = t // tq
    return pl.pallas_call(
        functools.partial(_attn_prompt_kernel, tq=tq, n_pairs=d_attn // (2 * hd), hd=hd),
        grid=(bsz, nq),
        in_specs=[pl.BlockSpec((tq, d_attn), lambda b, i: (b * nq + i, 0)),
                  pl.BlockSpec((t, d_attn), lambda b, i: (b, 0)),
                  pl.BlockSpec((t, d_attn), lambda b, i: (b, 0))],
        out_specs=pl.BlockSpec((tq, d_attn), lambda b, i: (b * nq + i, 0)),
        out_shape=jax.ShapeDtypeStruct((n, d_attn), F32),
        compiler_params=_params("arbitrary", "arbitrary"),
        name="attn_prompt",
    )(q, kb, vb)


def _attn_sample_kernel(pt_ref, q_ref, kn_ref, vn_ref, *rest, n_pages, page, n_heads, hd, ts):
    del pt_ref
    kp_refs = rest[:n_pages]
    vp_refs = rest[n_pages:2 * n_pages]
    o_ref = rest[2 * n_pages]
    rows = n_heads * ts
    d_attn = n_heads * hd
    ts_shift, hd_shift = _log2(ts), _log2(hd)

    head_mask = (_iota((rows, d_attn), 0) >> ts_shift) == (_iota((rows, d_attn), 1) >> hd_shift)
    q = q_ref[0]
    qbd = jnp.where(head_mask, jnp.concatenate([q] * n_heads, axis=0), 0.0).astype(BF16)
    u_tri = (_iota((page, page), 0) > _iota((page, page), 1)).astype(BF16)

    pad = jnp.zeros((page - ts, d_attn), F32)
    k_new = jnp.concatenate([kn_ref[0], pad], axis=0).astype(BF16)
    v_new = jnp.concatenate([vn_ref[0], pad], axis=0).astype(BF16)
    new_mask = _iota((rows, page), 1) < (_iota((rows, page), 0) & (ts - 1))
    w, carry = _sb_tile(_dot_nt(qbd, k_new), jnp.zeros((rows, 1), F32), u_tri, new_mask)
    acc = _dot(w.astype(BF16), v_new)
    for p in reversed(range(n_pages)):
        w, carry = _sb_tile(_dot(qbd, kp_refs[p][0, 0].astype(BF16)), carry, u_tri, None)
        acc = acc + _dot_nt(w.astype(BF16), vp_refs[p][0, 0].astype(BF16))

    y = jnp.sum(jnp.where(head_mask, acc, 0.0).reshape(n_heads, ts, d_attn), axis=0)
    sq = y * y
    col_head = _iota((ts, d_attn), 1) >> hd_shift
    scale = jnp.zeros((ts, d_attn), F32)
    for h in range(n_heads):
        sel = col_head == h
        ms = jnp.sum(jnp.where(sel, sq, 0.0), axis=-1, keepdims=True) * (1.0 / hd)
        scale = jnp.where(sel, lax.rsqrt(ms + RMS_EPS), scale)
    o_ref[0] = y * scale


def _attn_sample(q, k_new, v_new, cache_k, cache_v, page_table_flat, layer, *, n_heads, hd):
    bsz, ts, d_attn = q.shape
    n_pages = page_table_flat.shape[0] // bsz
    page = cache_k.shape[3]
    assert ts <= page and ts % SUBLANES == 0

    def page_spec(p):
        return pl.BlockSpec((1, 1, d_attn, page), lambda b, pt: (layer, pt[b * n_pages + p], 0, 0))

    tok_spec = pl.BlockSpec((1, ts, d_attn), lambda b, pt: (b, 0, 0))
    grid_spec = pltpu.PrefetchScalarGridSpec(
        num_scalar_prefetch=1,
        grid=(bsz,),
        in_specs=[tok_spec, tok_spec, tok_spec]
        + [page_spec(p) for p in range(n_pages)] + [page_spec(p) for p in range(n_pages)],
        out_specs=tok_spec,
    )
    return pl.pallas_call(
        functools.partial(_attn_sample_kernel, n_pages=n_pages, page=page, n_heads=n_heads, hd=hd, ts=ts),
        grid_spec=grid_spec,
        out_shape=jax.ShapeDtypeStruct((bsz, ts, d_attn), F32),
        compiler_params=_params("arbitrary"),
        name="attn_sample",
    )(page_table_flat, q, k_new, v_new, *([cache_k] * n_pages), *([cache_v] * n_pages))


def _gelu_tanh(x):
    return 0.5 * x * (1.0 + jnp.tanh(math.sqrt(2.0 / math.pi) * (x + 0.044715 * (x * x * x))))


def _s5_kernel(u_ref, h0re_ref, h0im_ref, lamre_ref, lamim_ref, logdt_ref, braw_re_ref, braw_im_ref,
               cre_ref, cim_ref, d_ref, wglu_ref, bglu_ref,
               y_ref, hre_out_ref, him_out_ref,
               bu_re, bu_im, hs_re, hs_im, a_re_s, a_im_s, bbd_re, bbd_im,
               *, nb, tt, n_steps):
    step = pl.program_id(0)

    @pl.when(step == 0)
    def _():
        lam_re = lamre_ref[...]
        lam_im = lamim_ref[...]
        dt = jnp.exp(logdt_ref[...])
        mag = jnp.exp(lam_re * dt)
        a_re = mag * jnp.cos(lam_im * dt)
        a_im = mag * jnp.sin(lam_im * dt)
        den = lam_re * lam_re + lam_im * lam_im
        z_re = ((a_re - 1.0) * lam_re + a_im * lam_im) / den
        z_im = (a_im * lam_re - (a_re - 1.0) * lam_im) / den
        a_re_s[...] = a_re
        a_im_s[...] = a_im
        bbd_re[...] = (z_re * braw_re_ref[...] - z_im * braw_im_ref[...]).astype(BF16)
        bbd_im[...] = (z_re * braw_im_ref[...] + z_im * braw_re_ref[...]).astype(BF16)
        hs_re[...] = h0re_ref[...]
        hs_im[...] = h0im_ref[...]

    d_ssm = u_ref.shape[-1]
    u = u_ref[...].reshape(nb * tt, d_ssm)
    ub = u.astype(BF16)
    n_lb = bu_re.shape[0]
    lanes = [slice(j * LANES, (j + 1) * LANES) for j in range(n_lb)]
    for j in range(n_lb):
        bu_re[j] = _dot(ub, bbd_re[:, lanes[j]])
        bu_im[j] = _dot(ub, bbd_im[:, lanes[j]])

    a_re = [a_re_s[:, lanes[j]] for j in range(n_lb)]
    a_im = [a_im_s[:, lanes[j]] for j in range(n_lb)]
    for b0 in range(0, nb, SUBLANES):
        gs = min(SUBLANES, nb - b0)

        def scan_step(t, carry):
            idx = pl.ds(b0 * tt + t, gs, stride=tt)
            out = []
            for j in range(n_lb):
                h_re, h_im = carry[2 * j], carry[2 * j + 1]
                n_re = a_re[j] * h_re - a_im[j] * h_im + bu_re[j, idx, :]
                n_im = a_re[j] * h_im + a_im[j] * h_re + bu_im[j, idx, :]
                bu_re[j, idx, :] = n_re
                bu_im[j, idx, :] = n_im
                out += [n_re, n_im]
            return tuple(out)

        init = []
        for j in range(n_lb):
            init += [hs_re[b0:b0 + gs, lanes[j]], hs_im[b0:b0 + gs, lanes[j]]]
        fin = lax.fori_loop(0, tt, scan_step, tuple(init), unroll=min(tt, 8))
        for j in range(n_lb):
            hs_re[b0:b0 + gs, lanes[j]] = fin[2 * j]
            hs_im[b0:b0 + gs, lanes[j]] = fin[2 * j + 1]

    y = d_ref[...] * u
    for j in range(n_lb):
        y = y + (_dot(bu_re[j].astype(BF16), cre_ref[lanes[j], :])
                 - _dot(bu_im[j].astype(BF16), cim_ref[lanes[j], :]))
    g = _gelu_tanh(y)
    ys = g * jax.nn.sigmoid(_dot(g.astype(BF16), wglu_ref[...]) + bglu_ref[...])
    ys = ys * lax.rsqrt(jnp.mean(ys * ys, axis=-1, keepdims=True) + RMS_EPS)
    y_ref[...] = ys.reshape(nb, tt, d_ssm)

    @pl.when(step == n_steps - 1)
    def _():
        hre_out_ref[...] = hs_re[...]
        him_out_ref[...] = hs_im[...]


def _s5(u, h0_re, h0_im, lw):
    nb, t, d_ssm = u.shape
    n_state = h0_re.shape[-1]
    tt = _pick_tile(t, (512, 256, 128, 64, 8))
    n_steps = t // tt
    consts = (lw["lam_re"], lw["lam_im"], lw["log_dt"], lw["braw_re"], lw["braw_im"],
              lw["c_re_bd"], lw["c_im_bd"], lw["d_skip"], lw["w_glu"], lw["b_glu"])
    u_spec = pl.BlockSpec((nb, tt, d_ssm), lambda i: (0, i, 0))
    y, h_re, h_im = pl.pallas_call(
        functools.partial(_s5_kernel, nb=nb, tt=tt, n_steps=n_steps),
        grid=(n_steps,),
        in_specs=[u_spec, _const_spec(h0_re.shape), _const_spec(h0_im.shape)]
        + [_const_spec(c.shape) for c in consts],
        out_specs=[u_spec, _const_spec(h0_re.shape), _const_spec(h0_im.shape)],
        out_shape=[jax.ShapeDtypeStruct(u.shape, F32),
                   jax.ShapeDtypeStruct(h0_re.shape, F32), jax.ShapeDtypeStruct(h0_im.shape, F32)],
        scratch_shapes=[pltpu.VMEM((n_state // LANES, nb * tt, LANES), F32),
                        pltpu.VMEM((n_state // LANES, nb * tt, LANES), F32),
                        pltpu.VMEM((nb, n_state), F32), pltpu.VMEM((nb, n_state), F32),
                        pltpu.VMEM((1, n_state), F32), pltpu.VMEM((1, n_state), F32),
                        pltpu.VMEM((d_ssm, n_state), BF16), pltpu.VMEM((d_ssm, n_state), BF16)],
        compiler_params=_params("arbitrary"),
        name="s5",
    )(u, h0_re, h0_im, *consts)
    return y, h_re, h_im


def _mlstm_kernel(q_ref, k_ref, v_ref, o_ref, g_ref, c0_ref, n0_ref, m0_ref,
                  y_ref, c1_ref, n1_ref, m1_ref, c_s, n_s, m_s,
                  *, bb, chunk, n_heads, hd, n_chunks):
    ci = pl.program_id(1)

    @pl.when(ci == 0)
    def _():
        c_s[...] = c0_ref[...]
        n_s[...] = n0_ref[...]
        m_s[...] = m0_ref[...]

    L = chunk
    lane = _iota((L, LANES), 1)
    tril = _iota((L, L), 0) >= _iota((L, L), 1)
    tril_b = tril.astype(BF16)
    triu_b = (_iota((L, L), 0) <= _iota((L, L), 1)).astype(BF16)
    pick = (_iota((SUBLANES, LANES), 0) == _iota((SUBLANES, LANES), 1)).astype(BF16)
    k_scale = hd ** -0.5

    for b in range(bb):
        gates = g_ref[b]
        is_f = (lane >= n_heads) & (lane < 2 * n_heads)
        gl = jnp.where(is_f, jax.nn.log_sigmoid(gates), gates)
        parts = _split3(gl)
        cum_col = sum(_dot(tril_b, p) for p in parts)
        g_row = sum(_dot_nt(pick, p) for p in parts)
        cum_row = sum(_dot(p, triu_b) for p in _split3(g_row))

        outs = []
        for h in range(n_heads):
            sl = slice(h * hd, (h + 1) * hd)
            q = q_ref[b, :, sl]
            k = k_ref[b, :, sl] * k_scale
            v = v_ref[b, :, sl]
            qb, kb = q.astype(BF16), k.astype(BF16)
            c_prev = c_s[b, h]
            n_prev = n_s[b, h]
            m_prev = m_s[b, h]

            i_col = gl[:, h:h + 1]
            b_col = cum_col[:, n_heads + h:n_heads + h + 1]
            i_row = g_row[h:h + 1, :]
            b_row = cum_row[n_heads + h:n_heads + h + 1, :]

            log_d = jnp.where(tril, b_col - b_row + i_row, -jnp.inf)
            log_inter = b_col + m_prev
            m_t = jnp.maximum(log_inter, jnp.max(log_d, axis=-1, keepdims=True))
            s = _dot_nt(qb, kb) * jnp.exp(log_d - m_t)
            w_inter = jnp.exp(log_inter - m_t)
            num = _dot(s.astype(BF16), v.astype(BF16)) + w_inter * _dot_nt(qb, c_prev.astype(BF16))
            den = jnp.sum(s, axis=-1, keepdims=True) + w_inter * jnp.sum(q * n_prev, axis=-1, keepdims=True)
            h_t = num / jnp.maximum(jnp.abs(den), jnp.exp(-m_t))

            m_new = m_t[L - 1:L, :]
            b_last = b_col[L - 1:L, :]
            w_s = jnp.exp(b_last - b_col + i_col - m_new)
            decay = jnp.exp(b_last + m_prev - m_new)
            c_s[b, h] = decay * c_prev + _dot_tn((w_s * v).astype(BF16), kb)
            n_s[b, h] = decay * n_prev + jnp.sum(w_s * k, axis=0, keepdims=True)
            m_s[b, h] = m_new

            y = h_t * jax.nn.sigmoid(o_ref[b, :, sl])
            outs.append(y * lax.rsqrt(jnp.mean(y * y, axis=-1, keepdims=True) + RMS_EPS))
        y_ref[b] = jnp.concatenate(outs, axis=-1)

    @pl.when(ci == n_chunks - 1)
    def _():
        c1_ref[...] = c_s[...]
        n1_ref[...] = n_s[...]
        m1_ref[...] = m_s[...]


def _mlstm(q, k, v, o, gates, c0, n0, m0, *, hd):
    bsz, t, d_ml = q.shape
    n_heads = d_ml // hd
    chunk = _pick_tile(t, (128, 64, 8))
    n_chunks = t // chunk
    bb = 1 if n_chunks > 1 else _pick_tile(bsz, (4, 2, 1))
    tok = lambda w: pl.BlockSpec((bb, chunk, w), lambda b, c: (b, c, 0))
    st = lambda shape: pl.BlockSpec((bb,) + shape[1:], lambda b, c: (b,) + (0,) * (len(shape) - 1))
    y, c1, n1, m1 = pl.pallas_call(
        functools.partial(_mlstm_kernel, bb=bb, chunk=chunk, n_heads=n_heads, hd=hd, n_chunks=n_chunks),
        grid=(bsz // bb, n_chunks),
        in_specs=[tok(d_ml), tok(d_ml), tok(d_ml), tok(d_ml), tok(LANES),
                  st(c0.shape), st(n0.shape), st(m0.shape)],
        out_specs=[tok(d_ml), st(c0.shape), st(n0.shape), st(m0.shape)],
        out_shape=[jax.ShapeDtypeStruct(q.shape, F32), jax.ShapeDtypeStruct(c0.shape, F32),
                   jax.ShapeDtypeStruct(n0.shape, F32), jax.ShapeDtypeStruct(m0.shape, F32)],
        scratch_shapes=[pltpu.VMEM((bb,) + c0.shape[1:], F32), pltpu.VMEM((bb,) + n0.shape[1:], F32),
                        pltpu.VMEM((bb,) + m0.shape[1:], F32)],
        compiler_params=_params("arbitrary", "arbitrary"),
        name="mlstm",
    )(q, k, v, o, gates, c0, n0, m0)
    return y, c1, n1, m1


def _layer_norm(x, g, b):
    mu = jnp.mean(x, axis=-1, keepdims=True)
    xc = x - mu
    var = jnp.mean(xc * xc, axis=-1, keepdims=True)
    return xc * lax.rsqrt(var + LN_EPS) * g + b


def _dense_kernel(x_ref, ya_ref, ys_ref, ym_ref, mixg_ref, wout_ref, ln1g_ref, ln1b_ref,
                  wgate_ref, wup_ref, wdown_ref, ln2g_ref, ln2b_ref, o_ref, *, alpha, ff_chunks):
    d_a = ya_ref.shape[-1]
    d_s = ys_ref.shape[-1]
    d_m = ym_ref.shape[-1]
    mixed = (_dot((ya_ref[...] * mixg_ref[:, :d_a]).astype(BF16), wout_ref[:d_a, :])
             + _dot((ys_ref[...] * mixg_ref[:, d_a:d_a + d_s]).astype(BF16), wout_ref[d_a:d_a + d_s, :])
             + _dot((ym_ref[...] * mixg_ref[:, d_a + d_s:]).astype(BF16), wout_ref[d_a + d_s:d_a + d_s + d_m, :]))
    x1 = _layer_norm(alpha * x_ref[...] + mixed, ln1g_ref[...], ln1b_ref[...])
    x1b = x1.astype(BF16)
    f = jnp.zeros_like(x1)
    c0 = 0
    for width in ff_chunks:
        gate = _dot(x1b, wgate_ref[:, c0:c0 + width])
        up = _dot(x1b, wup_ref[:, c0:c0 + width])
        hid = (gate * jax.nn.sigmoid(gate) * up).astype(BF16)
        f = f + _dot(hid, wdown_ref[c0:c0 + width, :])
        c0 += width
    o_ref[...] = _layer_norm(alpha * x1 + f, ln2g_ref[...], ln2b_ref[...])


def _dense(x, ya, ys, ym, lw, *, alpha):
    n, d = x.shape
    tm = _pick_tile(n, (256, 128, 64))
    d_ff = lw["w_gate"].shape[1]
    ff_chunks = []
    left = d_ff
    while left > 0:
        ff_chunks.append(min(1024, left))
        left -= ff_chunks[-1]
    row = lambda w: pl.BlockSpec((tm, w), lambda i: (i, 0))
    consts = (lw["mix_g"], lw["w_out"], lw["ln1_g"], lw["ln1_b"], lw["w_gate"], lw["w_up"], lw["w_down"],
              lw["ln2_g"], lw["ln2_b"])
    return pl.pallas_call(
        functools.partial(_dense_kernel, alpha=alpha, ff_chunks=tuple(ff_chunks)),
        grid=(n // tm,),
        in_specs=[row(d), row(ya.shape[1]), row(ys.shape[1]), row(ym.shape[1])]
        + [_const_spec(c.shape) for c in consts],
        out_specs=row(d),
        out_shape=jax.ShapeDtypeStruct((n, d), F32),
        compiler_params=_params("arbitrary"),
        name="dense",
    )(x, ya, ys, ym, *consts)


def _block_diag(blocks):
    g, r, c = blocks.shape
    eye = jnp.eye(g, dtype=blocks.dtype)
    return (eye[:, None, :, None] * blocks[:, :, None, :]).reshape(g * r, g * c)


def kernel(x_prompt, x_sample, cache_k, cache_v, page_table, state_ssm_re, state_ssm_im, state_mlstm_c, state_mlstm_n, state_mlstm_m, w_in, b_in, mlstm_gate_b, mix_g, w_out, ln1_g, ln1_b, ssm_lambda_re, ssm_lambda_im, ssm_log_dt, ssm_b_re, ssm_b_im, ssm_c_re, ssm_c_im, ssm_d, ssm_w_glu, ssm_b_glu, w_gate, w_up, w_down, ln2_g, ln2_b):
    depth = w_in.shape[0]
    bp, tp, d_model = x_prompt.shape
    bs, ts, _ = x_sample.shape
    _, n_pool, page, h_a, hd_a = cache_k.shape
    d_attn = h_a * hd_a
    _, n_groups, n_p = ssm_lambda_re.shape
    cg = ssm_b_re.shape[-1]
    d_ssm = n_groups * cg
    n_state = n_groups * n_p
    _, _, h_m, hd_m, _ = state_mlstm_c.shape
    d_ml = h_m * hd_m
    d_main = 3 * d_attn + d_ssm + 4 * d_ml
    assert w_in.shape[2] == d_main + 2 * h_m and 2 * h_m <= SUBLANES
    q_scale = hd_a ** -0.5
    assert 2.0 ** round(math.log2(q_scale)) == q_scale
    alpha = (2 * depth) ** 0.25

    cache_k2 = jnp.transpose(cache_k, (0, 1, 3, 4, 2)).reshape(depth, n_pool, d_attn, page)
    cache_v2 = jnp.transpose(cache_v, (0, 1, 3, 4, 2)).reshape(depth, n_pool, d_attn, page)
    pt_flat = page_table.reshape(-1).astype(jnp.int32)
    row = lambda a: a.reshape(1, -1).astype(F32)

    xp = x_prompt.reshape(bp * tp, d_model)
    xs = x_sample.reshape(bs * ts, d_model)
    zero_state = jnp.zeros((bp, n_state), F32)
    zero_c = jnp.zeros((bp, h_m, hd_m, hd_m), F32)
    zero_n = jnp.zeros((bp, h_m, 1, hd_m), F32)
    zero_m = jnp.zeros((bp, h_m, 1, 1), F32)

    outs_p = [[] for _ in range(7)]
    outs_s = [[] for _ in range(7)]
    for l in range(depth):
        gate_pad = LANES - 2 * h_m
        lw = {
            "w_main": w_in[l, :, :d_main].astype(BF16),
            "b_main": row(b_in[l, :d_main]),
            "w_gate_in": jnp.pad(w_in[l, :, d_main:], ((0, 0), (0, gate_pad))).astype(BF16),
            "b_gate_in": row(jnp.pad(b_in[l, d_main:] + mlstm_gate_b[l], (0, gate_pad))),
            "lam_re": row(ssm_lambda_re[l]),
            "lam_im": row(ssm_lambda_im[l]),
            "log_dt": row(jnp.repeat(ssm_log_dt[l], n_p)),
            "braw_re": _block_diag(jnp.swapaxes(ssm_b_re[l], 1, 2)).astype(F32),
            "braw_im": _block_diag(jnp.swapaxes(ssm_b_im[l], 1, 2)).astype(F32),
            "c_re_bd": _block_diag(jnp.swapaxes(ssm_c_re[l], 1, 2)).astype(BF16),
            "c_im_bd": _block_diag(jnp.swapaxes(ssm_c_im[l], 1, 2)).astype(BF16),
            "d_skip": row(ssm_d[l]),
            "w_glu": ssm_w_glu[l].astype(BF16),
            "b_glu": row(ssm_b_glu[l]),
            "mix_g": row(mix_g[l]),
            "w_out": w_out[l].astype(BF16),
            "ln1_g": row(ln1_g[l]), "ln1_b": row(ln1_b[l]),
            "w_gate": w_gate[l].astype(BF16), "w_up": w_up[l].astype(BF16), "w_down": w_down[l].astype(BF16),
            "ln2_g": row(ln2_g[l]), "ln2_b": row(ln2_b[l]),
        }
        proj = functools.partial(_proj, w_main=lw["w_main"], b_main=lw["b_main"], w_gate=lw["w_gate_in"],
                                 b_gate=lw["b_gate_in"], d_attn=d_attn, d_ssm=d_ssm, d_ml=d_ml, q_scale=q_scale)

        q, k, v, kb, vb, us, qm, km, vm, om, gates = proj(xp)
        ya = _attn_prompt(q, kb, vb, bsz=bp, t=tp, hd=hd_a)
        ys, h_re, h_im = _s5(us.reshape(bp, tp, d_ssm), zero_state, zero_state, lw)
        r3 = lambda a: a.reshape(bp, tp, -1)
        ym, c1, n1, m1 = _mlstm(r3(qm), r3(km), r3(vm), r3(om), r3(gates), zero_c, zero_n, zero_m, hd=hd_m)
        xp = _dense(xp, ya, ys.reshape(bp * tp, d_ssm), ym.reshape(bp * tp, d_ml), lw, alpha=alpha)
        for lst, a in zip(outs_p, (k, v, h_re, h_im, c1, n1, m1)):
            lst.append(a)

        q, k, v, kb, vb, us, qm, km, vm, om, gates = proj(xs)
        r3 = lambda a: a.reshape(bs, ts, -1)
        ya = _attn_sample(r3(q), r3(k), r3(v), cache_k2, cache_v2, pt_flat, l, n_heads=h_a, hd=hd_a)
        ys, h_re, h_im = _s5(r3(us), state_ssm_re[l].reshape(bs, n_state), state_ssm_im[l].reshape(bs, n_state), lw)
        ym, c1, n1, m1 = _mlstm(r3(qm), r3(km), r3(vm), r3(om), r3(gates), state_mlstm_c[l],
                                state_mlstm_n[l].reshape(bs, h_m, 1, hd_m), state_mlstm_m[l].reshape(bs, h_m, 1, 1),
                                hd=hd_m)
        xs = _dense(xs, ya.reshape(bs * ts, d_attn), ys.reshape(bs * ts, d_ssm), ym.reshape(bs * ts, d_ml), lw,
                    alpha=alpha)
        for lst, a in zip(outs_s, (k, v, h_re, h_im, c1, n1, m1)):
            lst.append(a)

    st = lambda lst: jnp.stack(lst)
    k_prompt = st(outs_p[0]).reshape(depth, bp, tp // page, page, h_a, hd_a)
    v_prompt = st(outs_p[1]).reshape(depth, bp, tp // page, page, h_a, hd_a)
    k_sample = st(outs_s[0]).reshape(depth, bs, ts, h_a, hd_a)
    v_sample = st(outs_s[1]).reshape(depth, bs, ts, h_a, hd_a)
    ssm = lambda lst, b: st(lst).reshape(depth, b, n_groups, n_p)
    return (xp.reshape(bp, tp, d_model), xs.reshape(bs, ts, d_model),
            k_prompt, v_prompt, k_sample, v_sample,
            ssm(outs_p[2], bp), ssm(outs_p[3], bp), ssm(outs_s[2], bs), ssm(outs_s[3], bs),
            st(outs_p[4]), st(outs_p[5]).reshape(depth, bp, h_m, hd_m), st(outs_p[6]).reshape(depth, bp, h_m),
            st(outs_s[4]), st(outs_s[5]).reshape(depth, bs, h_m, hd_m), st(outs_s[6]).reshape(depth, bs, h_m))
```

```python
import functools
import math

import jax
import jax.numpy as jnp
from jax import lax
from jax.experimental import pallas as pl
from jax.experimental.pallas import tpu as pltpu

F32 = jnp.float32
BF16 = jnp.bfloat16

LN_EPS = 1e-5
RMS_EPS = 1e-6
V7X_VMEM_LIMIT_BYTES = 56 * 1024 * 1024
LANES = 128
SUBLANES = 8


def _dot(a, b):
    return jnp.dot(a, b, preferred_element_type=F32)


def _dot_nt(a, b):
    return lax.dot_general(a, b, (((1,), (1,)), ((), ())), preferred_element_type=F32)


def _dot_tn(a, b):
    return lax.dot_general(a, b, (((0,), (0,)), ((), ())), preferred_element_type=F32)


def _split2(x):
    hi = x.astype(BF16)
    lo = (x - hi.astype(F32)).astype(BF16)
    return hi, lo


def _later_sum_matrix(k):
    return ((_iota((2 * k, k), 0) & (k - 1)) > _iota((2 * k, k), 1)).astype(BF16)


def _later_sum(x, u2):
    return _dot(jnp.concatenate(_split2(x), axis=-1), u2)


def _split3(x):
    hi = x.astype(BF16)
    r = x - hi.astype(F32)
    mid = r.astype(BF16)
    lo = (r - mid.astype(F32)).astype(BF16)
    return hi, mid, lo


def _iota(shape, dim):
    return lax.broadcasted_iota(jnp.int32, shape, dim)


def _log2(n):
    k = int(math.log2(n))
    assert 1 << k == n, n
    return k


def _pick_tile(n, candidates):
    for c in candidates:
        if n % c == 0:
            return c
    raise ValueError(f"no tile for {n} in {candidates}")


def _params(*sem):
    return pltpu.CompilerParams(dimension_semantics=sem, vmem_limit_bytes=V7X_VMEM_LIMIT_BYTES)


def _const_spec(shape):
    nd = len(shape)
    return pl.BlockSpec(shape, lambda *_: (0,) * nd)


def _proj_kernel(x_ref, w_ref, b_ref, wg_ref, bg_ref, *rest, d_attn, d_ssm, d_ml, q_scale, paged):
    xb = x_ref[...].astype(BF16)

    def cols(c0, width):
        return _dot(xb, w_ref[:, c0:c0 + width]) + b_ref[:, c0:c0 + width]

    if paged:
        wkvt_ref, bkvt_ref, q_ref, kp_ref, vp_ref, kpb_ref, vpb_ref = rest[:7]
        kvt = _dot_nt(wkvt_ref[...], xb) + bkvt_ref[...]
        page = kp_ref.shape[-1]
        for j in range(kp_ref.shape[0]):
            kt = kvt[:d_attn, j * page:(j + 1) * page]
            vt = kvt[d_attn:, j * page:(j + 1) * page]
            kp_ref[j] = kt
            vp_ref[j] = vt
            kpb_ref[j] = kt.astype(BF16)
            vpb_ref[j] = vt.astype(BF16)
        rest = rest[7:]
    else:
        q_ref, k_ref, v_ref = rest[:3]
        k_ref[...] = cols(d_attn, d_attn)
        v_ref[...] = cols(2 * d_attn, d_attn)
        rest = rest[3:]
    us_ref, qm_ref, km_ref, vm_ref, om_ref, g_ref = rest
    q_ref[...] = cols(0, d_attn) * q_scale
    c = 3 * d_attn
    us_ref[...] = cols(c, d_ssm)
    c += d_ssm
    for ref in (qm_ref, km_ref, vm_ref, om_ref):
        ref[...] = cols(c, d_ml)
        c += d_ml
    g_ref[...] = _dot(xb, wg_ref[...]) + bg_ref[...]


def _proj(x, lw, *, d_attn, d_ssm, d_ml, q_scale, page=None):
    n, d = x.shape
    tm = _pick_tile(n, (512, 256, 128, 64))
    row = lambda w, dt=F32: (pl.BlockSpec((tm, w), lambda i: (i, 0)), jax.ShapeDtypeStruct((n, w), dt))
    ins = [x, lw["w_main"], lw["b_main"], lw["w_gate_in"], lw["b_gate_in"]]
    if page is None:
        outs = [row(d_attn), row(d_attn), row(d_attn)]
    else:
        assert tm % page == 0
        ins += [lw["w_kvt"], lw["b_kvt"]]
        slab = lambda dt: (pl.BlockSpec((tm // page, d_attn, page), lambda i: (i, 0, 0)),
                           jax.ShapeDtypeStruct((n // page, d_attn, page), dt))
        outs = [row(d_attn), slab(F32), slab(F32), slab(BF16), slab(BF16)]
    outs += [row(d_ssm), row(d_ml), row(d_ml), row(d_ml), row(d_ml), row(LANES)]
    return pl.pallas_call(
        functools.partial(_proj_kernel, d_attn=d_attn, d_ssm=d_ssm, d_ml=d_ml, q_scale=q_scale,
                          paged=page is not None),
        grid=(n // tm,),
        in_specs=[pl.BlockSpec((tm, d), lambda i: (i, 0))] + [_const_spec(a.shape) for a in ins[1:]],
        out_specs=[o[0] for o in outs],
        out_shape=[o[1] for o in outs],
        compiler_params=_params("arbitrary"),
        name="proj",
    )(*ins)


def _sb_weights(zs, carries, u2, masks):
    raws, sps = [], []
    for z, m in zip(zs, masks):
        raw = jnp.maximum(z, 0.0) + jnp.log(1.0 + jnp.exp(-jnp.abs(z)))
        raws.append(raw)
        sps.append(raw if m is None else jnp.where(m, raw, 0.0))
    cums = [_later_sum(sp, u2) for sp in sps]
    ws, new_carries = [], []
    for z, raw, sp, cum, carry, m in zip(zs, raws, sps, cums, carries, masks):
        w = jnp.exp((z - raw) - cum - carry)
        ws.append(w if m is None else jnp.where(m, w, 0.0))
        new_carries.append(carry + jnp.sum(sp, axis=-1, keepdims=True))
    return ws, new_carries


def _attn_prompt_kernel(q_ref, k_ref, v_ref, o_ref, *, tq, tk, n_heads, hd):
    qi = pl.program_id(1)
    page = k_ref.shape[-1]
    ppb = tk // page
    u2 = _later_sum_matrix(tk)
    heads = [slice(h * hd, (h + 1) * hd) for h in range(n_heads)]
    q = q_ref[...]
    qs = [q[:, sl].astype(BF16) for sl in heads]

    def block(jb, carries, accs, masks):
        def kv(ref, sl):
            return jnp.concatenate([ref[jb * ppb + i, sl, :] for i in range(ppb)], axis=-1)
        zs = [_dot(qs[h], kv(k_ref, heads[h])) for h in range(n_heads)]
        ws, carries = _sb_weights(zs, carries, u2, masks)
        accs = [acc + _dot_nt(w.astype(BF16), kv(v_ref, heads[h]))
                for h, (acc, w) in enumerate(zip(accs, ws))]
        return carries, accs

    jd = (qi * tq) // tk
    offset = qi * tq - jd * tk
    diag_mask = _iota((tq, tk), 1) < _iota((tq, tk), 0) + offset
    carries = [jnp.zeros((tq, 1), F32)] * n_heads
    accs = [jnp.zeros((tq, hd), F32)] * n_heads
    carries, accs = block(jd, carries, accs, [diag_mask] * n_heads)

    def body(i, ca):
        carries, accs = block(jd - 1 - i, list(ca[0]), list(ca[1]), [None] * n_heads)
        return tuple(carries), tuple(accs)

    carries, accs = lax.fori_loop(0, jd, body, (tuple(carries), tuple(accs)))
    outs = [acc * lax.rsqrt(jnp.mean(acc * acc, axis=-1, keepdims=True) + RMS_EPS) for acc in accs]
    o_ref[...] = jnp.concatenate(outs, axis=-1)


def _attn_prompt(q, kpages, vpages, *, bsz, t, hd):
    n, d_attn = q.shape
    n_pages, _, page = kpages.shape
    assert n == bsz * t and n_pages * page == n
    tq = _pick_tile(t, (128, 64))
    tk = _pick_tile(t, (256, 128))
    assert tk % tq == 0 and tk % page == 0
    nq = t // tq
    ppb = t // page
    return pl.pallas_call(
        functools.partial(_attn_prompt_kernel, tq=tq, tk=tk, n_heads=d_attn // hd, hd=hd),
        grid=(bsz, nq),
        in_specs=[pl.BlockSpec((tq, d_attn), lambda b, i: (b * nq + i, 0)),
                  pl.BlockSpec((ppb, d_attn, page), lambda b, i: (b, 0, 0)),
                  pl.BlockSpec((ppb, d_attn, page), lambda b, i: (b, 0, 0))],
        out_specs=pl.BlockSpec((tq, d_attn), lambda b, i: (b * nq + i, 0)),
        out_shape=jax.ShapeDtypeStruct((n, d_attn), F32),
        compiler_params=_params("arbitrary", "arbitrary"),
        name="attn_prompt",
    )(q, kpages, vpages)


def _attn_sample_kernel(pt_ref, q_ref, kn_ref, vn_ref, *rest, n_pages, page, n_heads, hd, ts):
    del pt_ref
    kp_refs = rest[:n_pages]
    vp_refs = rest[n_pages:2 * n_pages]
    o_ref = rest[2 * n_pages]
    rows = n_heads * ts
    d_attn = n_heads * hd
    ts_shift, hd_shift = _log2(ts), _log2(hd)

    head_mask = (_iota((rows, d_attn), 0) >> ts_shift) == (_iota((rows, d_attn), 1) >> hd_shift)
    q = q_ref[0]
    qbd = jnp.where(head_mask, jnp.concatenate([q] * n_heads, axis=0), 0.0).astype(BF16)
    u2 = _later_sum_matrix(page)

    pad = jnp.zeros((page - ts, d_attn), F32)
    k_new = jnp.concatenate([kn_ref[0], pad], axis=0).astype(BF16)
    v_new = jnp.concatenate([vn_ref[0], pad], axis=0).astype(BF16)
    new_mask = _iota((rows, page), 1) < (_iota((rows, page), 0) & (ts - 1))
    zero = jnp.zeros((rows, 1), F32)
    (w,), (carry,) = _sb_weights([_dot_nt(qbd, k_new)], [zero], u2, [new_mask])
    acc = _dot(w.astype(BF16), v_new)
    order = list(reversed(range(n_pages)))
    zs = [_dot(qbd, kp_refs[p][0, 0].astype(BF16)) for p in order]
    raws = [jnp.maximum(z, 0.0) + jnp.log(1.0 + jnp.exp(-jnp.abs(z))) for z in zs]
    cums = [_later_sum(raw, u2) for raw in raws]
    for z, raw, cum, p in zip(zs, raws, cums, order):
        w = jnp.exp((z - raw) - cum - carry)
        carry = carry + jnp.sum(raw, axis=-1, keepdims=True)
        acc = acc + _dot_nt(w.astype(BF16), vp_refs[p][0, 0].astype(BF16))

    y = jnp.sum(jnp.where(head_mask, acc, 0.0).reshape(n_heads, ts, d_attn), axis=0)
    sq = y * y
    col_head = _iota((ts, d_attn), 1) >> hd_shift
    scale = jnp.zeros((ts, d_attn), F32)
    for h in range(n_heads):
        sel = col_head == h
        ms = jnp.sum(jnp.where(sel, sq, 0.0), axis=-1, keepdims=True) * (1.0 / hd)
        scale = jnp.where(sel, lax.rsqrt(ms + RMS_EPS), scale)
    o_ref[0] = y * scale


def _attn_sample(q, k_new, v_new, cache_k, cache_v, page_table_flat, layer, *, n_heads, hd):
    bsz, ts, d_attn = q.shape
    n_pages = page_table_flat.shape[0] // bsz
    page = cache_k.shape[3]
    assert ts <= page and ts % SUBLANES == 0

    def page_spec(p):
        return pl.BlockSpec((1, 1, d_attn, page), lambda b, pt: (layer, pt[b * n_pages + p], 0, 0))

    tok_spec = pl.BlockSpec((1, ts, d_attn), lambda b, pt: (b, 0, 0))
    grid_spec = pltpu.PrefetchScalarGridSpec(
        num_scalar_prefetch=1,
        grid=(bsz,),
        in_specs=[tok_spec, tok_spec, tok_spec]
        + [page_spec(p) for p in range(n_pages)] + [page_spec(p) for p in range(n_pages)],
        out_specs=tok_spec,
    )
    return pl.pallas_call(
        functools.partial(_attn_sample_kernel, n_pages=n_pages, page=page, n_heads=n_heads, hd=hd, ts=ts),
        grid_spec=grid_spec,
        out_shape=jax.ShapeDtypeStruct((bsz, ts, d_attn), F32),
        compiler_params=_params("arbitrary"),
        name="attn_sample",
    )(page_table_flat, q, k_new, v_new, *([cache_k] * n_pages), *([cache_v] * n_pages))


def _gelu_tanh(x):
    return 0.5 * x * (1.0 + jnp.tanh(math.sqrt(2.0 / math.pi) * (x + 0.044715 * (x * x * x))))


def _s5_kernel(u_ref, h0re_ref, h0im_ref, lamre_ref, lamim_ref, logdt_ref, braw_re_ref, braw_im_ref,
               cre_ref, cim_ref, d_ref, wglu_ref, bglu_ref,
               y_ref, hre_out_ref, him_out_ref,
               bu_re, bu_im, hs_re, hs_im, a_re_s, a_im_s, bbd_re, bbd_im,
               *, nb, tt, n_steps):
    step = pl.program_id(0)

    @pl.when(step == 0)
    def _():
        lam_re = lamre_ref[...]
        lam_im = lamim_ref[...]
        dt = jnp.exp(logdt_ref[...])
        mag = jnp.exp(lam_re * dt)
        a_re = mag * jnp.cos(lam_im * dt)
        a_im = mag * jnp.sin(lam_im * dt)
        den = lam_re * lam_re + lam_im * lam_im
        z_re = ((a_re - 1.0) * lam_re + a_im * lam_im) / den
        z_im = (a_im * lam_re - (a_re - 1.0) * lam_im) / den
        a_re_s[...] = a_re
        a_im_s[...] = a_im
        bbd_re[...] = (z_re * braw_re_ref[...] - z_im * braw_im_ref[...]).astype(BF16)
        bbd_im[...] = (z_re * braw_im_ref[...] + z_im * braw_re_ref[...]).astype(BF16)
        hs_re[...] = h0re_ref[...]
        hs_im[...] = h0im_ref[...]

    d_ssm = u_ref.shape[-1]
    u = u_ref[...].reshape(nb * tt, d_ssm)
    ub = u.astype(BF16)
    n_lb = bu_re.shape[0]
    lanes = [slice(j * LANES, (j + 1) * LANES) for j in range(n_lb)]
    for j in range(n_lb):
        bu_re[j] = _dot(ub, bbd_re[:, lanes[j]])
        bu_im[j] = _dot(ub, bbd_im[:, lanes[j]])

    a_re = [a_re_s[:, lanes[j]] for j in range(n_lb)]
    a_im = [a_im_s[:, lanes[j]] for j in range(n_lb)]
    for b0 in range(0, nb, SUBLANES):
        gs = min(SUBLANES, nb - b0)

        def scan_step(t, carry):
            idx = pl.ds(b0 * tt + t, gs, stride=tt)
            out = []
            for j in range(n_lb):
                h_re, h_im = carry[2 * j], carry[2 * j + 1]
                n_re = a_re[j] * h_re - a_im[j] * h_im + bu_re[j, idx, :]
                n_im = a_re[j] * h_im + a_im[j] * h_re + bu_im[j, idx, :]
                bu_re[j, idx, :] = n_re
                bu_im[j, idx, :] = n_im
                out += [n_re, n_im]
            return tuple(out)

        init = []
        for j in range(n_lb):
            init += [hs_re[b0:b0 + gs, lanes[j]], hs_im[b0:b0 + gs, lanes[j]]]
        fin = lax.fori_loop(0, tt, scan_step, tuple(init), unroll=min(tt, 8))
        for j in range(n_lb):
            hs_re[b0:b0 + gs, lanes[j]] = fin[2 * j]
            hs_im[b0:b0 + gs, lanes[j]] = fin[2 * j + 1]

    y = d_ref[...] * u
    for j in range(n_lb):
        y = y + (_dot(bu_re[j].astype(BF16), cre_ref[lanes[j], :])
                 - _dot(bu_im[j].astype(BF16), cim_ref[lanes[j], :]))
    g = _gelu_tanh(y)
    ys = g * jax.nn.sigmoid(_dot(g.astype(BF16), wglu_ref[...]) + bglu_ref[...])
    ys = ys * lax.rsqrt(jnp.mean(ys * ys, axis=-1, keepdims=True) + RMS_EPS)
    y_ref[...] = ys.reshape(nb, tt, d_ssm)

    @pl.when(step == n_steps - 1)
    def _():
        hre_out_ref[...] = hs_re[...]
        him_out_ref[...] = hs_im[...]


def _s5(u, h0_re, h0_im, lw):
    nb, t, d_ssm = u.shape
    n_state = h0_re.shape[-1]
    tt = _pick_tile(t, (512, 256, 128, 64, 8))
    n_steps = t // tt
    consts = (lw["lam_re"], lw["lam_im"], lw["log_dt"], lw["braw_re"], lw["braw_im"],
              lw["c_re_bd"], lw["c_im_bd"], lw["d_skip"], lw["w_glu"], lw["b_glu"])
    u_spec = pl.BlockSpec((nb, tt, d_ssm), lambda i: (0, i, 0))
    y, h_re, h_im = pl.pallas_call(
        functools.partial(_s5_kernel, nb=nb, tt=tt, n_steps=n_steps),
        grid=(n_steps,),
        in_specs=[u_spec, _const_spec(h0_re.shape), _const_spec(h0_im.shape)]
        + [_const_spec(c.shape) for c in consts],
        out_specs=[u_spec, _const_spec(h0_re.shape), _const_spec(h0_im.shape)],
        out_shape=[jax.ShapeDtypeStruct(u.shape, F32),
                   jax.ShapeDtypeStruct(h0_re.shape, F32), jax.ShapeDtypeStruct(h0_im.shape, F32)],
        scratch_shapes=[pltpu.VMEM((n_state // LANES, nb * tt, LANES), F32),
                        pltpu.VMEM((n_state // LANES, nb * tt, LANES), F32),
                        pltpu.VMEM((nb, n_state), F32), pltpu.VMEM((nb, n_state), F32),
                        pltpu.VMEM((1, n_state), F32), pltpu.VMEM((1, n_state), F32),
                        pltpu.VMEM((d_ssm, n_state), BF16), pltpu.VMEM((d_ssm, n_state), BF16)],
        compiler_params=_params("arbitrary"),
        name="s5",
    )(u, h0_re, h0_im, *consts)
    return y, h_re, h_im


def _mlstm_kernel(q_ref, k_ref, v_ref, o_ref, g_ref, c0_ref, n0_ref, m0_ref,
                  y_ref, c1_ref, n1_ref, m1_ref, c_s, n_s, m_s,
                  *, bb, chunk, n_heads, hd, n_chunks):
    ci = pl.program_id(1)

    @pl.when(ci == 0)
    def _():
        c_s[...] = c0_ref[...]
        n_s[...] = n0_ref[...]
        m_s[...] = m0_ref[...]

    L = chunk
    lane = _iota((L, LANES), 1)
    tril = _iota((L, L), 0) >= _iota((L, L), 1)
    tril_b = tril.astype(BF16)
    triu_b = (_iota((L, L), 0) <= _iota((L, L), 1)).astype(BF16)
    pick = (_iota((SUBLANES, LANES), 0) == _iota((SUBLANES, LANES), 1)).astype(BF16)
    k_scale = hd ** -0.5

    for b in range(bb):
        gates = g_ref[b]
        is_f = (lane >= n_heads) & (lane < 2 * n_heads)
        gl = jnp.where(is_f, jax.nn.log_sigmoid(gates), gates)
        parts = _split3(gl)
        cum_col = sum(_dot(tril_b, p) for p in parts)
        g_row = sum(_dot_nt(pick, p) for p in parts)
        cum_row = sum(_dot(p, triu_b) for p in _split3(g_row))

        outs = []
        for h in range(n_heads):
            sl = slice(h * hd, (h + 1) * hd)
            q = q_ref[b, :, sl]
            k = k_ref[b, :, sl] * k_scale
            v = v_ref[b, :, sl]
            qb, kb = q.astype(BF16), k.astype(BF16)
            c_prev = c_s[b, h]
            n_prev = n_s[b, h]
            m_prev = m_s[b, h]

            i_col = gl[:, h:h + 1]
            b_col = cum_col[:, n_heads + h:n_heads + h + 1]
            i_row = g_row[h:h + 1, :]
            b_row = cum_row[n_heads + h:n_heads + h + 1, :]

            log_d = jnp.where(tril, b_col - b_row + i_row, -jnp.inf)
            log_inter = b_col + m_prev
            m_t = jnp.maximum(log_inter, jnp.max(log_d, axis=-1, keepdims=True))
            s = _dot_nt(qb, kb) * jnp.exp(log_d - m_t)
            w_inter = jnp.exp(log_inter - m_t)
            num = _dot(s.astype(BF16), v.astype(BF16)) + w_inter * _dot_nt(qb, c_prev.astype(BF16))
            den = jnp.sum(s, axis=-1, keepdims=True) + w_inter * jnp.sum(q * n_prev, axis=-1, keepdims=True)
            h_t = num / jnp.maximum(jnp.abs(den), jnp.exp(-m_t))

            m_new = m_t[L - 1:L, :]
            b_last = b_col[L - 1:L, :]
            w_s = jnp.exp(b_last - b_col + i_col - m_new)
            decay = jnp.exp(b_last + m_prev - m_new)
            c_s[b, h] = decay * c_prev + _dot_tn((w_s * v).astype(BF16), kb)
            n_s[b, h] = decay * n_prev + jnp.sum(w_s * k, axis=0, keepdims=True)
            m_s[b, h] = m_new

            y = h_t * jax.nn.sigmoid(o_ref[b, :, sl])
            outs.append(y * lax.rsqrt(jnp.mean(y * y, axis=-1, keepdims=True) + RMS_EPS))
        y_ref[b] = jnp.concatenate(outs, axis=-1)

    @pl.when(ci == n_chunks - 1)
    def _():
        c1_ref[...] = c_s[...]
        n1_ref[...] = n_s[...]
        m1_ref[...] = m_s[...]


def _mlstm(q, k, v, o, gates, c0, n0, m0, *, hd):
    bsz, t, d_ml = q.shape
    n_heads = d_ml // hd
    chunk = _pick_tile(t, (128, 64, 8))
    n_chunks = t // chunk
    bb = 1 if n_chunks > 1 else _pick_tile(bsz, (4, 2, 1))
    tok = lambda w: pl.BlockSpec((bb, chunk, w), lambda b, c: (b, c, 0))
    st = lambda shape: pl.BlockSpec((bb,) + shape[1:], lambda b, c: (b,) + (0,) * (len(shape) - 1))
    y, c1, n1, m1 = pl.pallas_call(
        functools.partial(_mlstm_kernel, bb=bb, chunk=chunk, n_heads=n_heads, hd=hd, n_chunks=n_chunks),
        grid=(bsz // bb, n_chunks),
        in_specs=[tok(d_ml), tok(d_ml), tok(d_ml), tok(d_ml), tok(LANES),
                  st(c0.shape), st(n0.shape), st(m0.shape)],
        out_specs=[tok(d_ml), st(c0.shape), st(n0.shape), st(m0.shape)],
        out_shape=[jax.ShapeDtypeStruct(q.shape, F32), jax.ShapeDtypeStruct(c0.shape, F32),
                   jax.ShapeDtypeStruct(n0.shape, F32), jax.ShapeDtypeStruct(m0.shape, F32)],
        scratch_shapes=[pltpu.VMEM((bb,) + c0.shape[1:], F32), pltpu.VMEM((bb,) + n0.shape[1:], F32),
                        pltpu.VMEM((bb,) + m0.shape[1:], F32)],
        compiler_params=_params("arbitrary", "arbitrary"),
        name="mlstm",
    )(q, k, v, o, gates, c0, n0, m0)
    return y, c1, n1, m1


def _layer_norm(x, g, b):
    mu = jnp.mean(x, axis=-1, keepdims=True)
    xc = x - mu
    var = jnp.mean(xc * xc, axis=-1, keepdims=True)
    return xc * lax.rsqrt(var + LN_EPS) * g + b


def _dense_kernel(x_ref, ya_ref, ys_ref, ym_ref, mixg_ref, wout_ref, ln1g_ref, ln1b_ref,
                  wgate_ref, wup_ref, wdown_ref, ln2g_ref, ln2b_ref, o_ref, *, alpha, ff_chunks):
    d_a = ya_ref.shape[-1]
    d_s = ys_ref.shape[-1]
    d_m = ym_ref.shape[-1]
    mixed = (_dot((ya_ref[...] * mixg_ref[:, :d_a]).astype(BF16), wout_ref[:d_a, :])
             + _dot((ys_ref[...] * mixg_ref[:, d_a:d_a + d_s]).astype(BF16), wout_ref[d_a:d_a + d_s, :])
             + _dot((ym_ref[...] * mixg_ref[:, d_a + d_s:]).astype(BF16), wout_ref[d_a + d_s:d_a + d_s + d_m, :]))
    x1 = _layer_norm(alpha * x_ref[...] + mixed, ln1g_ref[...], ln1b_ref[...])
    x1b = x1.astype(BF16)
    f = jnp.zeros_like(x1)
    c0 = 0
    for width in ff_chunks:
        gate = _dot(x1b, wgate_ref[:, c0:c0 + width])
        up = _dot(x1b, wup_ref[:, c0:c0 + width])
        hid = (gate * jax.nn.sigmoid(gate) * up).astype(BF16)
        f = f + _dot(hid, wdown_ref[c0:c0 + width, :])
        c0 += width
    o_ref[...] = _layer_norm(alpha * x1 + f, ln2g_ref[...], ln2b_ref[...])


def _dense(x, ya, ys, ym, lw, *, alpha):
    n, d = x.shape
    tm = _pick_tile(n, (256, 128, 64))
    d_ff = lw["w_gate"].shape[1]
    ff_chunks = []
    left = d_ff
    while left > 0:
        ff_chunks.append(min(1024, left))
        left -= ff_chunks[-1]
    row = lambda w: pl.BlockSpec((tm, w), lambda i: (i, 0))
    consts = (lw["mix_g"], lw["w_out"], lw["ln1_g"], lw["ln1_b"], lw["w_gate"], lw["w_up"], lw["w_down"],
              lw["ln2_g"], lw["ln2_b"])
    return pl.pallas_call(
        functools.partial(_dense_kernel, alpha=alpha, ff_chunks=tuple(ff_chunks)),
        grid=(n // tm,),
        in_specs=[row(d), row(ya.shape[1]), row(ys.shape[1]), row(ym.shape[1])]
        + [_const_spec(c.shape) for c in consts],
        out_specs=row(d),
        out_shape=jax.ShapeDtypeStruct((n, d), F32),
        compiler_params=_params("arbitrary"),
        name="dense",
    )(x, ya, ys, ym, *consts)


def _block_diag(blocks):
    g, r, c = blocks.shape
    eye = jnp.eye(g, dtype=blocks.dtype)
    return (eye[:, None, :, None] * blocks[:, :, None, :]).reshape(g * r, g * c)


def kernel(x_prompt, x_sample, cache_k, cache_v, page_table, state_ssm_re, state_ssm_im, state_mlstm_c, state_mlstm_n, state_mlstm_m, w_in, b_in, mlstm_gate_b, mix_g, w_out, ln1_g, ln1_b, ssm_lambda_re, ssm_lambda_im, ssm_log_dt, ssm_b_re, ssm_b_im, ssm_c_re, ssm_c_im, ssm_d, ssm_w_glu, ssm_b_glu, w_gate, w_up, w_down, ln2_g, ln2_b):
    depth = w_in.shape[0]
    bp, tp, d_model = x_prompt.shape
    bs, ts, _ = x_sample.shape
    _, n_pool, page, h_a, hd_a = cache_k.shape
    d_attn = h_a * hd_a
    _, n_groups, n_p = ssm_lambda_re.shape
    cg = ssm_b_re.shape[-1]
    d_ssm = n_groups * cg
    n_state = n_groups * n_p
    _, _, h_m, hd_m, _ = state_mlstm_c.shape
    d_ml = h_m * hd_m
    d_main = 3 * d_attn + d_ssm + 4 * d_ml
    assert w_in.shape[2] == d_main + 2 * h_m and 2 * h_m <= SUBLANES
    q_scale = hd_a ** -0.5
    assert 2.0 ** round(math.log2(q_scale)) == q_scale
    alpha = (2 * depth) ** 0.25

    cache_k2 = jnp.transpose(cache_k, (0, 1, 3, 4, 2)).reshape(depth, n_pool, d_attn, page)
    cache_v2 = jnp.transpose(cache_v, (0, 1, 3, 4, 2)).reshape(depth, n_pool, d_attn, page)
    pt_flat = page_table.reshape(-1).astype(jnp.int32)
    row = lambda a: a.reshape(1, -1).astype(F32)

    xp = x_prompt.reshape(bp * tp, d_model)
    xs = x_sample.reshape(bs * ts, d_model)
    zero_state = jnp.zeros((bp, n_state), F32)
    zero_c = jnp.zeros((bp, h_m, hd_m, hd_m), F32)
    zero_n = jnp.zeros((bp, h_m, 1, hd_m), F32)
    zero_m = jnp.zeros((bp, h_m, 1, 1), F32)

    outs_p = [[] for _ in range(7)]
    outs_s = [[] for _ in range(7)]
    for l in range(depth):
        gate_pad = LANES - 2 * h_m
        lw = {
            "w_main": w_in[l, :, :d_main].astype(BF16),
            "b_main": row(b_in[l, :d_main]),
            "w_gate_in": jnp.pad(w_in[l, :, d_main:], ((0, 0), (0, gate_pad))).astype(BF16),
            "w_kvt": w_in[l, :, d_attn:3 * d_attn].T.astype(BF16),
            "b_kvt": b_in[l, d_attn:3 * d_attn].reshape(-1, 1).astype(F32),
            "b_gate_in": row(jnp.pad(b_in[l, d_main:] + mlstm_gate_b[l], (0, gate_pad))),
            "lam_re": row(ssm_lambda_re[l]),
            "lam_im": row(ssm_lambda_im[l]),
            "log_dt": row(jnp.repeat(ssm_log_dt[l], n_p)),
            "braw_re": _block_diag(jnp.swapaxes(ssm_b_re[l], 1, 2)).astype(F32),
            "braw_im": _block_diag(jnp.swapaxes(ssm_b_im[l], 1, 2)).astype(F32),
            "c_re_bd": _block_diag(jnp.swapaxes(ssm_c_re[l], 1, 2)).astype(BF16),
            "c_im_bd": _block_diag(jnp.swapaxes(ssm_c_im[l], 1, 2)).astype(BF16),
            "d_skip": row(ssm_d[l]),
            "w_glu": ssm_w_glu[l].astype(BF16),
            "b_glu": row(ssm_b_glu[l]),
            "mix_g": row(mix_g[l]),
            "w_out": w_out[l].astype(BF16),
            "ln1_g": row(ln1_g[l]), "ln1_b": row(ln1_b[l]),
            "w_gate": w_gate[l].astype(BF16), "w_up": w_up[l].astype(BF16), "w_down": w_down[l].astype(BF16),
            "ln2_g": row(ln2_g[l]), "ln2_b": row(ln2_b[l]),
        }
        proj = functools.partial(_proj, lw=lw, d_attn=d_attn, d_ssm=d_ssm, d_ml=d_ml, q_scale=q_scale)

        q, k, v, kb, vb, us, qm, km, vm, om, gates = proj(xp, page=page)
        ya = _attn_prompt(q, kb, vb, bsz=bp, t=tp, hd=hd_a)
        ys, h_re, h_im = _s5(us.reshape(bp, tp, d_ssm), zero_state, zero_state, lw)
        r3 = lambda a: a.reshape(bp, tp, -1)
        ym, c1, n1, m1 = _mlstm(r3(qm), r3(km), r3(vm), r3(om), r3(gates), zero_c, zero_n, zero_m, hd=hd_m)
        xp = _dense(xp, ya, ys.reshape(bp * tp, d_ssm), ym.reshape(bp * tp, d_ml), lw, alpha=alpha)
        for lst, a in zip(outs_p, (k, v, h_re, h_im, c1, n1, m1)):
            lst.append(a)

        q, k, v, us, qm, km, vm, om, gates = proj(xs)
        r3 = lambda a: a.reshape(bs, ts, -1)
        ya = _attn_sample(r3(q), r3(k), r3(v), cache_k2, cache_v2, pt_flat, l, n_heads=h_a, hd=hd_a)
        ys, h_re, h_im = _s5(r3(us), state_ssm_re[l].reshape(bs, n_state), state_ssm_im[l].reshape(bs, n_state), lw)
        ym, c1, n1, m1 = _mlstm(r3(qm), r3(km), r3(vm), r3(om), r3(gates), state_mlstm_c[l],
                                state_mlstm_n[l].reshape(bs, h_m, 1, hd_m), state_mlstm_m[l].reshape(bs, h_m, 1, 1),
                                hd=hd_m)
        xs = _dense(xs, ya.reshape(bs * ts, d_attn), ys.reshape(bs * ts, d_ssm), ym.reshape(bs * ts, d_ml), lw,
                    alpha=alpha)
        for lst, a in zip(outs_s, (k, v, h_re, h_im, c1, n1, m1)):
            lst.append(a)

    st = lambda lst: jnp.stack(lst)
    paged = lambda lst: jnp.transpose(st(lst).reshape(depth, bp, tp // page, h_a, hd_a, page), (0, 1, 2, 5, 3, 4))
    k_prompt = paged(outs_p[0])
    v_prompt = paged(outs_p[1])
    k_sample = st(outs_s[0]).reshape(depth, bs, ts, h_a, hd_a)
    v_sample = st(outs_s[1]).reshape(depth, bs, ts, h_a, hd_a)
    ssm = lambda lst, b: st(lst).reshape(depth, b, n_groups, n_p)
    return (xp.reshape(bp, tp, d_model), xs.reshape(bs, ts, d_model),
            k_prompt, v_prompt, k_sample, v_sample,
            ssm(outs_p[2], bp), ssm(outs_p[3], bp), ssm(outs_s[2], bs), ssm(outs_s[3], bs),
            st(outs_p[4]), st(outs_p[5]).reshape(depth, bp, h_m, hd_m), st(outs_p[6]).reshape(depth, bp, h_m),
            st(outs_s[4]), st(outs_s[5]).reshape(depth, bs, h_m, hd_m), st(outs_s[6]).reshape(depth, bs, h_m))
```

```python
import functools
import math

import jax
import jax.numpy as jnp
from jax import lax
from jax.experimental import pallas as pl
from jax.experimental.pallas import tpu as pltpu

F32 = jnp.float32
BF16 = jnp.bfloat16

LN_EPS = 1e-5
RMS_EPS = 1e-6
V7X_VMEM_LIMIT_BYTES = 56 * 1024 * 1024
LANES = 128
SUBLANES = 8


def _dot(a, b):
    return jnp.dot(a, b, preferred_element_type=F32)


def _dot_nt(a, b):
    return lax.dot_general(a, b, (((1,), (1,)), ((), ())), preferred_element_type=F32)


def _dot_tn(a, b):
    return lax.dot_general(a, b, (((0,), (0,)), ((), ())), preferred_element_type=F32)


def _split2(x):
    hi = x.astype(BF16)
    lo = (x - hi.astype(F32)).astype(BF16)
    return hi, lo


def _later_sum_matrix(k):
    return ((_iota((2 * k, k), 0) & (k - 1)) > _iota((2 * k, k), 1)).astype(BF16)


def _later_sum(x, u2):
    return _dot(jnp.concatenate(_split2(x), axis=-1), u2)


def _split3(x):
    hi = x.astype(BF16)
    r = x - hi.astype(F32)
    mid = r.astype(BF16)
    lo = (r - mid.astype(F32)).astype(BF16)
    return hi, mid, lo


def _iota(shape, dim):
    return lax.broadcasted_iota(jnp.int32, shape, dim)


def _log2(n):
    k = int(math.log2(n))
    assert 1 << k == n, n
    return k


def _pick_tile(n, candidates):
    for c in candidates:
        if n % c == 0:
            return c
    raise ValueError(f"no tile for {n} in {candidates}")


def _params(*sem):
    return pltpu.CompilerParams(dimension_semantics=sem, vmem_limit_bytes=V7X_VMEM_LIMIT_BYTES)


def _const_spec(shape):
    nd = len(shape)
    return pl.BlockSpec(shape, lambda *_: (0,) * nd)


def _proj_kernel(x_ref, w_ref, b_ref, wg_ref, bg_ref, *rest, d_attn, d_ssm, d_ml, q_scale, paged):
    xb = x_ref[...].astype(BF16)

    def cols(c0, width):
        return _dot(xb, w_ref[:, c0:c0 + width]) + b_ref[:, c0:c0 + width]

    if paged:
        wkvt_ref, bkvt_ref, q_ref, kp_ref, vp_ref, kpb_ref, vpb_ref = rest[:7]
        kvt = _dot_nt(wkvt_ref[...], xb) + bkvt_ref[...]
        page = kp_ref.shape[-1]
        for j in range(kp_ref.shape[0]):
            kt = kvt[:d_attn, j * page:(j + 1) * page]
            vt = kvt[d_attn:, j * page:(j + 1) * page]
            kp_ref[j] = kt
            vp_ref[j] = vt
            kpb_ref[j] = kt.astype(BF16)
            vpb_ref[j] = vt.astype(BF16)
        rest = rest[7:]
    else:
        q_ref, k_ref, v_ref = rest[:3]
        k_ref[...] = cols(d_attn, d_attn)
        v_ref[...] = cols(2 * d_attn, d_attn)
        rest = rest[3:]
    us_ref, qm_ref, km_ref, vm_ref, om_ref, g_ref = rest
    q_ref[...] = cols(0, d_attn) * q_scale
    c = 3 * d_attn
    us_ref[...] = cols(c, d_ssm)
    c += d_ssm
    for ref in (qm_ref, km_ref, vm_ref, om_ref):
        ref[...] = cols(c, d_ml)
        c += d_ml
    g_ref[...] = _dot(xb, wg_ref[...]) + bg_ref[...]


def _proj(x, lw, *, d_attn, d_ssm, d_ml, q_scale, page=None):
    n, d = x.shape
    tm = _pick_tile(n, (512, 256, 128, 64))
    row = lambda w, dt=F32: (pl.BlockSpec((tm, w), lambda i: (i, 0)), jax.ShapeDtypeStruct((n, w), dt))
    ins = [x, lw["w_main"], lw["b_main"], lw["w_gate_in"], lw["b_gate_in"]]
    if page is None:
        outs = [row(d_attn), row(d_attn), row(d_attn)]
    else:
        assert tm % page == 0
        ins += [lw["w_kvt"], lw["b_kvt"]]
        slab = lambda dt: (pl.BlockSpec((tm // page, d_attn, page), lambda i: (i, 0, 0)),
                           jax.ShapeDtypeStruct((n // page, d_attn, page), dt))
        outs = [row(d_attn), slab(F32), slab(F32), slab(BF16), slab(BF16)]
    outs += [row(d_ssm), row(d_ml), row(d_ml), row(d_ml), row(d_ml), row(LANES)]
    return pl.pallas_call(
        functools.partial(_proj_kernel, d_attn=d_attn, d_ssm=d_ssm, d_ml=d_ml, q_scale=q_scale,
                          paged=page is not None),
        grid=(n // tm,),
        in_specs=[pl.BlockSpec((tm, d), lambda i: (i, 0))] + [_const_spec(a.shape) for a in ins[1:]],
        out_specs=[o[0] for o in outs],
        out_shape=[o[1] for o in outs],
        compiler_params=_params("arbitrary"),
        name="proj",
    )(*ins)


def _sb_weights(zs, carries, u2, masks):
    raws, sps = [], []
    for z, m in zip(zs, masks):
        raw = jnp.maximum(z, 0.0) + jnp.log(1.0 + jnp.exp(-jnp.abs(z)))
        raws.append(raw)
        sps.append(raw if m is None else jnp.where(m, raw, 0.0))
    cums = [_later_sum(sp, u2) for sp in sps]
    ws, new_carries = [], []
    for z, raw, sp, cum, carry, m in zip(zs, raws, sps, cums, carries, masks):
        w = jnp.exp((z - raw) - cum - carry)
        ws.append(w if m is None else jnp.where(m, w, 0.0))
        new_carries.append(carry + jnp.sum(sp, axis=-1, keepdims=True))
    return ws, new_carries


def _attn_prompt_kernel(q_ref, k_ref, v_ref, o_ref, *, tq, tk, n_heads, hd):
    qi = pl.program_id(1)
    page = k_ref.shape[-1]
    ppb = tk // page
    u2 = _later_sum_matrix(tk)
    heads = [slice(h * hd, (h + 1) * hd) for h in range(n_heads)]
    q = q_ref[...]
    qs = [q[:, sl].astype(BF16) for sl in heads]

    def block(jb, carries, accs, masks):
        def kv(ref, sl):
            return jnp.concatenate([ref[jb * ppb + i, sl, :] for i in range(ppb)], axis=-1)
        zs = [_dot(qs[h], kv(k_ref, heads[h])) for h in range(n_heads)]
        ws, carries = _sb_weights(zs, carries, u2, masks)
        accs = [acc + _dot_nt(w.astype(BF16), kv(v_ref, heads[h]))
                for h, (acc, w) in enumerate(zip(accs, ws))]
        return carries, accs

    jd = (qi * tq) // tk
    offset = qi * tq - jd * tk
    diag_mask = _iota((tq, tk), 1) < _iota((tq, tk), 0) + offset
    carries = [jnp.zeros((tq, 1), F32)] * n_heads
    accs = [jnp.zeros((tq, hd), F32)] * n_heads
    carries, accs = block(jd, carries, accs, [diag_mask] * n_heads)

    def body(i, ca):
        carries, accs = block(jd - 1 - i, list(ca[0]), list(ca[1]), [None] * n_heads)
        return tuple(carries), tuple(accs)

    carries, accs = lax.fori_loop(0, jd, body, (tuple(carries), tuple(accs)))
    outs = [acc * lax.rsqrt(jnp.mean(acc * acc, axis=-1, keepdims=True) + RMS_EPS) for acc in accs]
    o_ref[...] = jnp.concatenate(outs, axis=-1)


def _attn_prompt(q, kpages, vpages, *, bsz, t, hd):
    n, d_attn = q.shape
    n_pages, _, page = kpages.shape
    assert n == bsz * t and n_pages * page == n
    tq = _pick_tile(t, (256, 128, 64))
    tk = _pick_tile(t, (256, 128))
    assert tk % tq == 0 and tk % page == 0
    nq = t // tq
    ppb = t // page
    return pl.pallas_call(
        functools.partial(_attn_prompt_kernel, tq=tq, tk=tk, n_heads=d_attn // hd, hd=hd),
        grid=(bsz, nq),
        in_specs=[pl.BlockSpec((tq, d_attn), lambda b, i: (b * nq + i, 0)),
                  pl.BlockSpec((ppb, d_attn, page), lambda b, i: (b, 0, 0)),
                  pl.BlockSpec((ppb, d_attn, page), lambda b, i: (b, 0, 0))],
        out_specs=pl.BlockSpec((tq, d_attn), lambda b, i: (b * nq + i, 0)),
        out_shape=jax.ShapeDtypeStruct((n, d_attn), F32),
        compiler_params=_params("arbitrary", "arbitrary"),
        name="attn_prompt",
    )(q, kpages, vpages)


def _attn_sample_kernel(pt_ref, q_ref, kn_ref, vn_ref, *rest, n_pages, page, n_heads, hd, ts):
    del pt_ref
    kp_refs = rest[:n_pages]
    vp_refs = rest[n_pages:2 * n_pages]
    o_ref = rest[2 * n_pages]
    rows = n_heads * ts
    d_attn = n_heads * hd
    ts_shift, hd_shift = _log2(ts), _log2(hd)

    head_mask = (_iota((rows, d_attn), 0) >> ts_shift) == (_iota((rows, d_attn), 1) >> hd_shift)
    q = q_ref[0]
    qbd = jnp.where(head_mask, jnp.concatenate([q] * n_heads, axis=0), 0.0).astype(BF16)
    u2 = _later_sum_matrix(page)

    pad = jnp.zeros((page - ts, d_attn), F32)
    k_new = jnp.concatenate([kn_ref[0], pad], axis=0).astype(BF16)
    v_new = jnp.concatenate([vn_ref[0], pad], axis=0).astype(BF16)
    new_mask = _iota((rows, page), 1) < (_iota((rows, page), 0) & (ts - 1))
    zero = jnp.zeros((rows, 1), F32)
    (w,), (carry,) = _sb_weights([_dot_nt(qbd, k_new)], [zero], u2, [new_mask])
    acc = _dot(w.astype(BF16), v_new)
    order = list(reversed(range(n_pages)))
    zs = [_dot(qbd, kp_refs[p][0, 0].astype(BF16)) for p in order]
    raws = [jnp.maximum(z, 0.0) + jnp.log(1.0 + jnp.exp(-jnp.abs(z))) for z in zs]
    cums = [_later_sum(raw, u2) for raw in raws]
    for z, raw, cum, p in zip(zs, raws, cums, order):
        w = jnp.exp((z - raw) - cum - carry)
        carry = carry + jnp.sum(raw, axis=-1, keepdims=True)
        acc = acc + _dot_nt(w.astype(BF16), vp_refs[p][0, 0].astype(BF16))

    y = jnp.sum(jnp.where(head_mask, acc, 0.0).reshape(n_heads, ts, d_attn), axis=0)
    sq = y * y
    col_head = _iota((ts, d_attn), 1) >> hd_shift
    scale = jnp.zeros((ts, d_attn), F32)
    for h in range(n_heads):
        sel = col_head == h
        ms = jnp.sum(jnp.where(sel, sq, 0.0), axis=-1, keepdims=True) * (1.0 / hd)
        scale = jnp.where(sel, lax.rsqrt(ms + RMS_EPS), scale)
    o_ref[0] = y * scale


def _attn_sample(q, k_new, v_new, cache_k, cache_v, page_table_flat, layer, *, n_heads, hd):
    bsz, ts, d_attn = q.shape
    n_pages = page_table_flat.shape[0] // bsz
    page = cache_k.shape[3]
    assert ts <= page and ts % SUBLANES == 0

    def page_spec(p):
        return pl.BlockSpec((1, 1, d_attn, page), lambda b, pt: (layer, pt[b * n_pages + p], 0, 0))

    tok_spec = pl.BlockSpec((1, ts, d_attn), lambda b, pt: (b, 0, 0))
    grid_spec = pltpu.PrefetchScalarGridSpec(
        num_scalar_prefetch=1,
        grid=(bsz,),
        in_specs=[tok_spec, tok_spec, tok_spec]
        + [page_spec(p) for p in range(n_pages)] + [page_spec(p) for p in range(n_pages)],
        out_specs=tok_spec,
    )
    return pl.pallas_call(
        functools.partial(_attn_sample_kernel, n_pages=n_pages, page=page, n_heads=n_heads, hd=hd, ts=ts),
        grid_spec=grid_spec,
        out_shape=jax.ShapeDtypeStruct((bsz, ts, d_attn), F32),
        compiler_params=_params("arbitrary"),
        name="attn_sample",
    )(page_table_flat, q, k_new, v_new, *([cache_k] * n_pages), *([cache_v] * n_pages))


def _gelu_tanh(x):
    return 0.5 * x * (1.0 + jnp.tanh(math.sqrt(2.0 / math.pi) * (x + 0.044715 * (x * x * x))))


def _s5_kernel(u_ref, h0re_ref, h0im_ref, lamre_ref, lamim_ref, logdt_ref, braw_re_ref, braw_im_ref,
               cre_ref, cim_ref, d_ref, wglu_ref, bglu_ref,
               y_ref, hre_out_ref, him_out_ref,
               bu_re, bu_im, hs_re, hs_im, a_re_s, a_im_s, bbd_re, bbd_im,
               *, nb, tt, n_steps):
    step = pl.program_id(0)

    @pl.when(step == 0)
    def _():
        lam_re = lamre_ref[...]
        lam_im = lamim_ref[...]
        dt = jnp.exp(logdt_ref[...])
        mag = jnp.exp(lam_re * dt)
        a_re = mag * jnp.cos(lam_im * dt)
        a_im = mag * jnp.sin(lam_im * dt)
        den = lam_re * lam_re + lam_im * lam_im
        z_re = ((a_re - 1.0) * lam_re + a_im * lam_im) / den
        z_im = (a_im * lam_re - (a_re - 1.0) * lam_im) / den
        a_re_s[...] = a_re
        a_im_s[...] = a_im
        bbd_re[...] = (z_re * braw_re_ref[...] - z_im * braw_im_ref[...]).astype(BF16)
        bbd_im[...] = (z_re * braw_im_ref[...] + z_im * braw_re_ref[...]).astype(BF16)
        hs_re[...] = h0re_ref[...]
        hs_im[...] = h0im_ref[...]

    d_ssm = u_ref.shape[-1]
    u = u_ref[...]
    ub = u.astype(BF16)
    n_lb = bu_re.shape[0]
    lanes = [slice(j * LANES, (j + 1) * LANES) for j in range(n_lb)]
    for buf, bbd in ((bu_re, bbd_re), (bu_im, bbd_im)):
        full = _dot(ub, bbd[...])
        for j in range(n_lb):
            buf[j] = full[:, lanes[j]]

    a_re = [a_re_s[:, lanes[j]] for j in range(n_lb)]
    a_im = [a_im_s[:, lanes[j]] for j in range(n_lb)]
    for b0 in range(0, nb, SUBLANES):
        gs = min(SUBLANES, nb - b0)

        def scan_step(t, carry):
            start = t * nb + b0
            rows = pl.ds(pl.multiple_of(start, SUBLANES) if gs == SUBLANES and nb % SUBLANES == 0 else start, gs)
            out = []
            for j in range(n_lb):
                h_re, h_im = carry[2 * j], carry[2 * j + 1]
                n_re = a_re[j] * h_re - a_im[j] * h_im + bu_re[j, rows, :]
                n_im = a_re[j] * h_im + a_im[j] * h_re + bu_im[j, rows, :]
                bu_re[j, rows, :] = n_re
                bu_im[j, rows, :] = n_im
                out += [n_re, n_im]
            return tuple(out)

        init = []
        for j in range(n_lb):
            init += [hs_re[b0:b0 + gs, lanes[j]], hs_im[b0:b0 + gs, lanes[j]]]
        fin = lax.fori_loop(0, tt, scan_step, tuple(init), unroll=min(tt, 8))
        for j in range(n_lb):
            hs_re[b0:b0 + gs, lanes[j]] = fin[2 * j]
            hs_im[b0:b0 + gs, lanes[j]] = fin[2 * j + 1]

    states = lambda buf: jnp.concatenate([buf[j] for j in range(n_lb)], axis=-1).astype(BF16)
    y = _dot(states(bu_re), cre_ref[...]) - _dot(states(bu_im), cim_ref[...]) + d_ref[...] * u
    g = _gelu_tanh(y)
    ys = g * jax.nn.sigmoid(_dot(g.astype(BF16), wglu_ref[...]) + bglu_ref[...])
    ys = ys * lax.rsqrt(jnp.mean(ys * ys, axis=-1, keepdims=True) + RMS_EPS)
    y_ref[...] = ys

    @pl.when(step == n_steps - 1)
    def _():
        hre_out_ref[...] = hs_re[...]
        him_out_ref[...] = hs_im[...]


def _s5(u, h0_re, h0_im, lw):
    nb, t, d_ssm = u.shape
    n_state = h0_re.shape[-1]
    tt = _pick_tile(t, (512, 256, 128, 64, 8))
    n_steps = t // tt
    consts = (lw["lam_re"], lw["lam_im"], lw["log_dt"], lw["braw_re"], lw["braw_im"],
              lw["c_re_bd"], lw["c_im_bd"], lw["d_skip"], lw["w_glu"], lw["b_glu"])
    u_tm = jnp.swapaxes(u, 0, 1).reshape(t * nb, d_ssm)
    u_spec = pl.BlockSpec((tt * nb, d_ssm), lambda i: (i, 0))
    y, h_re, h_im = pl.pallas_call(
        functools.partial(_s5_kernel, nb=nb, tt=tt, n_steps=n_steps),
        grid=(n_steps,),
        in_specs=[u_spec, _const_spec(h0_re.shape), _const_spec(h0_im.shape)]
        + [_const_spec(c.shape) for c in consts],
        out_specs=[u_spec, _const_spec(h0_re.shape), _const_spec(h0_im.shape)],
        out_shape=[jax.ShapeDtypeStruct(u_tm.shape, F32),
                   jax.ShapeDtypeStruct(h0_re.shape, F32), jax.ShapeDtypeStruct(h0_im.shape, F32)],
        scratch_shapes=[pltpu.VMEM((n_state // LANES, nb * tt, LANES), F32),
                        pltpu.VMEM((n_state // LANES, nb * tt, LANES), F32),
                        pltpu.VMEM((nb, n_state), F32), pltpu.VMEM((nb, n_state), F32),
                        pltpu.VMEM((1, n_state), F32), pltpu.VMEM((1, n_state), F32),
                        pltpu.VMEM((d_ssm, n_state), BF16), pltpu.VMEM((d_ssm, n_state), BF16)],
        compiler_params=_params("arbitrary"),
        name="s5",
    )(u_tm, h0_re, h0_im, *consts)
    return jnp.swapaxes(y.reshape(t, nb, d_ssm), 0, 1), h_re, h_im


def _run_units(units, interleave):
    if not interleave:
        for u in units:
            for _ in u:
                pass
        return
    while units:
        alive = []
        for u in units:
            try:
                next(u)
                alive.append(u)
            except StopIteration:
                pass
        units = alive


def _mlstm_kernel(q_ref, k_ref, v_ref, o_ref, g_ref, c0_ref, n0_ref, m0_ref,
                  y_ref, c1_ref, n1_ref, m1_ref, c_s, n_s, m_s,
                  *, bb, chunk, n_heads, hd, n_chunks, interleave):
    ci = pl.program_id(1)

    @pl.when(ci == 0)
    def _():
        c_s[...] = c0_ref[...]
        n_s[...] = n0_ref[...]
        m_s[...] = m0_ref[...]

    L = chunk
    lane = _iota((L, LANES), 1)
    tril = _iota((L, L), 0) >= _iota((L, L), 1)
    tril_b = tril.astype(BF16)
    triu_b = (_iota((L, L), 0) <= _iota((L, L), 1)).astype(BF16)
    pick = (_iota((SUBLANES, LANES), 0) == _iota((SUBLANES, LANES), 1)).astype(BF16)
    k_scale = hd ** -0.5

    is_f = (lane >= n_heads) & (lane < 2 * n_heads)

    def gate_sums(b, out):
        gates = g_ref[b]
        gl = jnp.where(is_f, jax.nn.log_sigmoid(gates), gates)
        parts = _split3(gl)
        cum_col = sum(_dot(tril_b, p) for p in parts)
        g_row = sum(_dot_nt(pick, p) for p in parts)
        yield
        cum_row = sum(_dot(p, triu_b) for p in _split3(g_row))
        out[b] = (gl, cum_col, g_row, cum_row)

    def head(b, h, sums, out):
        gl, cum_col, g_row, cum_row = sums[b]
        sl = slice(h * hd, (h + 1) * hd)
        q = q_ref[b, :, sl]
        k = k_ref[b, :, sl] * k_scale
        v = v_ref[b, :, sl]
        qb, kb = q.astype(BF16), k.astype(BF16)
        c_prev = c_s[b, h]
        n_prev = n_s[b, h]
        m_prev = m_s[b, h]
        qk = _dot_nt(qb, kb)
        qc = _dot_nt(qb, c_prev.astype(BF16))
        yield

        i_col = gl[:, h:h + 1]
        b_col = cum_col[:, n_heads + h:n_heads + h + 1]
        i_row = g_row[h:h + 1, :]
        b_row = cum_row[n_heads + h:n_heads + h + 1, :]
        log_d = jnp.where(tril, b_col - b_row + i_row, -jnp.inf)
        log_inter = b_col + m_prev
        m_t = jnp.maximum(log_inter, jnp.max(log_d, axis=-1, keepdims=True))
        s = qk * jnp.exp(log_d - m_t)
        w_inter = jnp.exp(log_inter - m_t)
        m_new = m_t[L - 1:L, :]
        b_last = b_col[L - 1:L, :]
        w_s = jnp.exp(b_last - b_col + i_col - m_new)
        decay = jnp.exp(b_last + m_prev - m_new)
        sv = _dot(s.astype(BF16), v.astype(BF16))
        vk = _dot_tn((w_s * v).astype(BF16), kb)
        yield

        num = sv + w_inter * qc
        den = jnp.sum(s, axis=-1, keepdims=True) + w_inter * jnp.sum(q * n_prev, axis=-1, keepdims=True)
        h_t = num / jnp.maximum(jnp.abs(den), jnp.exp(-m_t))
        c_s[b, h] = decay * c_prev + vk
        n_s[b, h] = decay * n_prev + jnp.sum(w_s * k, axis=0, keepdims=True)
        m_s[b, h] = m_new
        y = h_t * jax.nn.sigmoid(o_ref[b, :, sl])
        out[(b, h)] = y * lax.rsqrt(jnp.mean(y * y, axis=-1, keepdims=True) + RMS_EPS)

    sums, outs = {}, {}
    _run_units([gate_sums(b, sums) for b in range(bb)], interleave)
    _run_units([head(b, h, sums, outs) for b in range(bb) for h in range(n_heads)], interleave)
    for b in range(bb):
        y_ref[b] = jnp.concatenate([outs[(b, h)] for h in range(n_heads)], axis=-1)

    @pl.when(ci == n_chunks - 1)
    def _():
        c1_ref[...] = c_s[...]
        n1_ref[...] = n_s[...]
        m1_ref[...] = m_s[...]


def _mlstm(q, k, v, o, gates, c0, n0, m0, *, hd):
    bsz, t, d_ml = q.shape
    n_heads = d_ml // hd
    chunk = _pick_tile(t, (128, 64, 8))
    n_chunks = t // chunk
    bb = 1 if n_chunks > 1 else _pick_tile(bsz, (4, 2, 1))
    tok = lambda w: pl.BlockSpec((bb, chunk, w), lambda b, c: (b, c, 0))
    st = lambda shape: pl.BlockSpec((bb,) + shape[1:], lambda b, c: (b,) + (0,) * (len(shape) - 1))
    y, c1, n1, m1 = pl.pallas_call(
        functools.partial(_mlstm_kernel, bb=bb, chunk=chunk, n_heads=n_heads, hd=hd, n_chunks=n_chunks,
                          interleave=n_chunks == 1),
        grid=(bsz // bb, n_chunks),
        in_specs=[tok(d_ml), tok(d_ml), tok(d_ml), tok(d_ml), tok(LANES),
                  st(c0.shape), st(n0.shape), st(m0.shape)],
        out_specs=[tok(d_ml), st(c0.shape), st(n0.shape), st(m0.shape)],
        out_shape=[jax.ShapeDtypeStruct(q.shape, F32), jax.ShapeDtypeStruct(c0.shape, F32),
                   jax.ShapeDtypeStruct(n0.shape, F32), jax.ShapeDtypeStruct(m0.shape, F32)],
        scratch_shapes=[pltpu.VMEM((bb,) + c0.shape[1:], F32), pltpu.VMEM((bb,) + n0.shape[1:], F32),
                        pltpu.VMEM((bb,) + m0.shape[1:], F32)],
        compiler_params=_params("arbitrary", "arbitrary"),
        name="mlstm",
    )(q, k, v, o, gates, c0, n0, m0)
    return y, c1, n1, m1


def _layer_norm(x, g, b):
    mu = jnp.mean(x, axis=-1, keepdims=True)
    xc = x - mu
    var = jnp.mean(xc * xc, axis=-1, keepdims=True)
    return xc * lax.rsqrt(var + LN_EPS) * g + b


def _dense_kernel(x_ref, ya_ref, ys_ref, ym_ref, mixg_ref, wout_ref, ln1g_ref, ln1b_ref,
                  wgate_ref, wup_ref, wdown_ref, ln2g_ref, ln2b_ref, o_ref, *, alpha, ff_chunks):
    d_a = ya_ref.shape[-1]
    d_s = ys_ref.shape[-1]
    d_m = ym_ref.shape[-1]
    mixed = (_dot((ya_ref[...] * mixg_ref[:, :d_a]).astype(BF16), wout_ref[:d_a, :])
             + _dot((ys_ref[...] * mixg_ref[:, d_a:d_a + d_s]).astype(BF16), wout_ref[d_a:d_a + d_s, :])
             + _dot((ym_ref[...] * mixg_ref[:, d_a + d_s:]).astype(BF16), wout_ref[d_a + d_s:d_a + d_s + d_m, :]))
    x1 = _layer_norm(alpha * x_ref[...] + mixed, ln1g_ref[...], ln1b_ref[...])
    x1b = x1.astype(BF16)
    f = jnp.zeros_like(x1)
    c0 = 0
    for width in ff_chunks:
        gate = _dot(x1b, wgate_ref[:, c0:c0 + width])
        up = _dot(x1b, wup_ref[:, c0:c0 + width])
        hid = (gate * jax.nn.sigmoid(gate) * up).astype(BF16)
        f = f + _dot(hid, wdown_ref[c0:c0 + width, :])
        c0 += width
    o_ref[...] = _layer_norm(alpha * x1 + f, ln2g_ref[...], ln2b_ref[...])


def _dense(x, ya, ys, ym, lw, *, alpha):
    n, d = x.shape
    tm = _pick_tile(n, (256, 128, 64))
    d_ff = lw["w_gate"].shape[1]
    ff_chunks = []
    left = d_ff
    while left > 0:
        ff_chunks.append(min(1024, left))
        left -= ff_chunks[-1]
    row = lambda w: pl.BlockSpec((tm, w), lambda i: (i, 0))
    consts = (lw["mix_g"], lw["w_out"], lw["ln1_g"], lw["ln1_b"], lw["w_gate"], lw["w_up"], lw["w_down"],
              lw["ln2_g"], lw["ln2_b"])
    return pl.pallas_call(
        functools.partial(_dense_kernel, alpha=alpha, ff_chunks=tuple(ff_chunks)),
        grid=(n // tm,),
        in_specs=[row(d), row(ya.shape[1]), row(ys.shape[1]), row(ym.shape[1])]
        + [_const_spec(c.shape) for c in consts],
        out_specs=row(d),
        out_shape=jax.ShapeDtypeStruct((n, d), F32),
        compiler_params=_params("arbitrary"),
        name="dense",
    )(x, ya, ys, ym, *consts)


def _block_diag(blocks):
    g, r, c = blocks.shape
    eye = jnp.eye(g, dtype=blocks.dtype)
    return (eye[:, None, :, None] * blocks[:, :, None, :]).reshape(g * r, g * c)


def kernel(x_prompt, x_sample, cache_k, cache_v, page_table, state_ssm_re, state_ssm_im, state_mlstm_c, state_mlstm_n, state_mlstm_m, w_in, b_in, mlstm_gate_b, mix_g, w_out, ln1_g, ln1_b, ssm_lambda_re, ssm_lambda_im, ssm_log_dt, ssm_b_re, ssm_b_im, ssm_c_re, ssm_c_im, ssm_d, ssm_w_glu, ssm_b_glu, w_gate, w_up, w_down, ln2_g, ln2_b):
    depth = w_in.shape[0]
    bp, tp, d_model = x_prompt.shape
    bs, ts, _ = x_sample.shape
    _, n_pool, page, h_a, hd_a = cache_k.shape
    d_attn = h_a * hd_a
    _, n_groups, n_p = ssm_lambda_re.shape
    cg = ssm_b_re.shape[-1]
    d_ssm = n_groups * cg
    n_state = n_groups * n_p
    _, _, h_m, hd_m, _ = state_mlstm_c.shape
    d_ml = h_m * hd_m
    d_main = 3 * d_attn + d_ssm + 4 * d_ml
    assert w_in.shape[2] == d_main + 2 * h_m and 2 * h_m <= SUBLANES
    q_scale = hd_a ** -0.5
    assert 2.0 ** round(math.log2(q_scale)) == q_scale
    alpha = (2 * depth) ** 0.25

    cache_k2 = jnp.transpose(cache_k, (0, 1, 3, 4, 2)).reshape(depth, n_pool, d_attn, page)
    cache_v2 = jnp.transpose(cache_v, (0, 1, 3, 4, 2)).reshape(depth, n_pool, d_attn, page)
    pt_flat = page_table.reshape(-1).astype(jnp.int32)
    row = lambda a: a.reshape(1, -1).astype(F32)

    xp = x_prompt.reshape(bp * tp, d_model)
    xs = x_sample.reshape(bs * ts, d_model)
    zero_state = jnp.zeros((bp, n_state), F32)
    zero_c = jnp.zeros((bp, h_m, hd_m, hd_m), F32)
    zero_n = jnp.zeros((bp, h_m, 1, hd_m), F32)
    zero_m = jnp.zeros((bp, h_m, 1, 1), F32)

    outs_p = [[] for _ in range(7)]
    outs_s = [[] for _ in range(7)]
    for l in range(depth):
        gate_pad = LANES - 2 * h_m
        lw = {
            "w_main": w_in[l, :, :d_main].astype(BF16),
            "b_main": row(b_in[l, :d_main]),
            "w_gate_in": jnp.pad(w_in[l, :, d_main:], ((0, 0), (0, gate_pad))).astype(BF16),
            "w_kvt": w_in[l, :, d_attn:3 * d_attn].T.astype(BF16),
            "b_kvt": b_in[l, d_attn:3 * d_attn].reshape(-1, 1).astype(F32),
            "b_gate_in": row(jnp.pad(b_in[l, d_main:] + mlstm_gate_b[l], (0, gate_pad))),
            "lam_re": row(ssm_lambda_re[l]),
            "lam_im": row(ssm_lambda_im[l]),
            "log_dt": row(jnp.repeat(ssm_log_dt[l], n_p)),
            "braw_re": _block_diag(jnp.swapaxes(ssm_b_re[l], 1, 2)).astype(F32),
            "braw_im": _block_diag(jnp.swapaxes(ssm_b_im[l], 1, 2)).astype(F32),
            "c_re_bd": _block_diag(jnp.swapaxes(ssm_c_re[l], 1, 2)).astype(BF16),
            "c_im_bd": _block_diag(jnp.swapaxes(ssm_c_im[l], 1, 2)).astype(BF16),
            "d_skip": row(ssm_d[l]),
            "w_glu": ssm_w_glu[l].astype(BF16),
            "b_glu": row(ssm_b_glu[l]),
            "mix_g": row(mix_g[l]),
            "w_out": w_out[l].astype(BF16),
            "ln1_g": row(ln1_g[l]), "ln1_b": row(ln1_b[l]),
            "w_gate": w_gate[l].astype(BF16), "w_up": w_up[l].astype(BF16), "w_down": w_down[l].astype(BF16),
            "ln2_g": row(ln2_g[l]), "ln2_b": row(ln2_b[l]),
        }
        proj = functools.partial(_proj, lw=lw, d_attn=d_attn, d_ssm=d_ssm, d_ml=d_ml, q_scale=q_scale)

        q, k, v, kb, vb, us, qm, km, vm, om, gates = proj(xp, page=page)
        ya = _attn_prompt(q, kb, vb, bsz=bp, t=tp, hd=hd_a)
        ys, h_re, h_im = _s5(us.reshape(bp, tp, d_ssm), zero_state, zero_state, lw)
        r3 = lambda a: a.reshape(bp, tp, -1)
        ym, c1, n1, m1 = _mlstm(r3(qm), r3(km), r3(vm), r3(om), r3(gates), zero_c, zero_n, zero_m, hd=hd_m)
        xp = _dense(xp, ya, ys.reshape(bp * tp, d_ssm), ym.reshape(bp * tp, d_ml), lw, alpha=alpha)
        for lst, a in zip(outs_p, (k, v, h_re, h_im, c1, n1, m1)):
            lst.append(a)

        q, k, v, us, qm, km, vm, om, gates = proj(xs)
        r3 = lambda a: a.reshape(bs, ts, -1)
        ya = _attn_sample(r3(q), r3(k), r3(v), cache_k2, cache_v2, pt_flat, l, n_heads=h_a, hd=hd_a)
        ys, h_re, h_im = _s5(r3(us), state_ssm_re[l].reshape(bs, n_state), state_ssm_im[l].reshape(bs, n_state), lw)
        ym, c1, n1, m1 = _mlstm(r3(qm), r3(km), r3(vm), r3(om), r3(gates), state_mlstm_c[l],
                                state_mlstm_n[l].reshape(bs, h_m, 1, hd_m), state_mlstm_m[l].reshape(bs, h_m, 1, 1),
                                hd=hd_m)
        xs = _dense(xs, ya.reshape(bs * ts, d_attn), ys.reshape(bs * ts, d_ssm), ym.reshape(bs * ts, d_ml), lw,
                    alpha=alpha)
        for lst, a in zip(outs_s, (k, v, h_re, h_im, c1, n1, m1)):
            lst.append(a)

    st = lambda lst: jnp.stack(lst)
    paged = lambda lst: jnp.transpose(st(lst).reshape(depth, bp, tp // page, h_a, hd_a, page), (0, 1, 2, 5, 3, 4))
    k_prompt = paged(outs_p[0])
    v_prompt = paged(outs_p[1])
    k_sample = st(outs_s[0]).reshape(depth, bs, ts, h_a, hd_a)
    v_sample = st(outs_s[1]).reshape(depth, bs, ts, h_a, hd_a)
    ssm = lambda lst, b: st(lst).reshape(depth, b, n_groups, n_p)
    return (xp.reshape(bp, tp, d_model), xs.reshape(bs, ts, d_model),
            k_prompt, v_prompt, k_sample, v_sample,
            ssm(outs_p[2], bp), ssm(outs_p[3], bp), ssm(outs_s[2], bs), ssm(outs_s[3], bs),
            st(outs_p[4]), st(outs_p[5]).reshape(depth, bp, h_m, hd_m), st(outs_p[6]).reshape(depth, bp, h_m),
            st(outs_s[4]), st(outs_s[5]).reshape(depth, bs, h_m, hd_m), st(outs_s[6]).reshape(depth, bs, h_m))
```

```python
import functools
import math

import jax
import jax.numpy as jnp
from jax import lax
from jax.experimental import pallas as pl
from jax.experimental.pallas import tpu as pltpu

F32 = jnp.float32
BF16 = jnp.bfloat16

LN_EPS = 1e-5
RMS_EPS = 1e-6
V7X_VMEM_LIMIT_BYTES = 56 * 1024 * 1024
LANES = 128
SUBLANES = 8


def _dot(a, b):
    return jnp.dot(a, b, preferred_element_type=F32)


def _dot_nt(a, b):
    return lax.dot_general(a, b, (((1,), (1,)), ((), ())), preferred_element_type=F32)


def _dot_tn(a, b):
    return lax.dot_general(a, b, (((0,), (0,)), ((), ())), preferred_element_type=F32)


def _split2(x):
    hi = x.astype(BF16)
    lo = (x - hi.astype(F32)).astype(BF16)
    return hi, lo


def _later_sum_matrix(k):
    return ((_iota((2 * k, k), 0) & (k - 1)) > _iota((2 * k, k), 1)).astype(BF16)


def _later_sum(x, u2):
    return _dot(jnp.concatenate(_split2(x), axis=-1), u2)


def _split3(x):
    hi = x.astype(BF16)
    r = x - hi.astype(F32)
    mid = r.astype(BF16)
    lo = (r - mid.astype(F32)).astype(BF16)
    return hi, mid, lo


def _iota(shape, dim):
    return lax.broadcasted_iota(jnp.int32, shape, dim)


def _log2(n):
    k = int(math.log2(n))
    assert 1 << k == n, n
    return k


def _pick_tile(n, candidates):
    for c in candidates:
        if n % c == 0:
            return c
    raise ValueError(f"no tile for {n} in {candidates}")


def _params(*sem):
    return pltpu.CompilerParams(dimension_semantics=sem, vmem_limit_bytes=V7X_VMEM_LIMIT_BYTES)


def _const_spec(shape):
    nd = len(shape)
    return pl.BlockSpec(shape, lambda *_: (0,) * nd)


def _proj_kernel(x_ref, w_ref, b_ref, wg_ref, bg_ref, *rest, d_attn, d_ssm, d_ml, q_scale, paged):
    xb = x_ref[...].astype(BF16)

    def cols(c0, width):
        return _dot(xb, w_ref[:, c0:c0 + width]) + b_ref[:, c0:c0 + width]

    if paged:
        wkvt_ref, bkvt_ref, q_ref, kp_ref, vp_ref, kpb_ref, vpb_ref = rest[:7]
        kvt = _dot_nt(wkvt_ref[...], xb) + bkvt_ref[...]
        page = kp_ref.shape[-1]
        for j in range(kp_ref.shape[0]):
            kt = kvt[:d_attn, j * page:(j + 1) * page]
            vt = kvt[d_attn:, j * page:(j + 1) * page]
            kp_ref[j] = kt
            vp_ref[j] = vt
            kpb_ref[j] = kt.astype(BF16)
            vpb_ref[j] = vt.astype(BF16)
        rest = rest[7:]
    else:
        q_ref, k_ref, v_ref = rest[:3]
        k_ref[...] = cols(d_attn, d_attn)
        v_ref[...] = cols(2 * d_attn, d_attn)
        rest = rest[3:]
    q_ref[...] = cols(0, d_attn) * q_scale
    c = 3 * d_attn
    rest[0][...] = cols(c, d_ssm)
    if len(rest) > 1:
        qm_ref, km_ref, vm_ref, om_ref, g_ref = rest[1:]
        c += d_ssm
        for ref in (qm_ref, km_ref, vm_ref, om_ref):
            ref[...] = cols(c, d_ml)
            c += d_ml
        g_ref[...] = _dot(xb, wg_ref[...]) + bg_ref[...]


def _proj(x, lw, *, d_attn, d_ssm, d_ml, q_scale, seq_len=None, page=None):
    n, d = x.shape
    tm = _pick_tile(n, (512, 256, 128, 64))
    row = lambda w, dt=F32: (pl.BlockSpec((tm, w), lambda i: (i, 0)), jax.ShapeDtypeStruct((n, w), dt))
    ins = [x, lw["w_main"], lw["b_main"], lw["w_gate_in"], lw["b_gate_in"]]
    if page is None:
        outs = [row(d_attn), row(d_attn), row(d_attn), row(d_ssm)]
    else:
        assert tm % page == 0 and seq_len % tm == 0
        ins += [lw["w_kvt"], lw["b_kvt"]]
        slab = lambda dt: (pl.BlockSpec((tm // page, d_attn, page), lambda i: (i, 0, 0)),
                           jax.ShapeDtypeStruct((n // page, d_attn, page), dt))
        tps = seq_len // tm
        us_tm = (pl.BlockSpec((tm, d_ssm), lambda i: (i % tps, i // tps)),
                 jax.ShapeDtypeStruct((seq_len, n // seq_len * d_ssm), F32))
        outs = [row(d_attn), slab(F32), slab(F32), slab(BF16), slab(BF16), us_tm,
                row(d_ml), row(d_ml), row(d_ml), row(d_ml), row(LANES)]
    return pl.pallas_call(
        functools.partial(_proj_kernel, d_attn=d_attn, d_ssm=d_ssm, d_ml=d_ml, q_scale=q_scale,
                          paged=page is not None),
        grid=(n // tm,),
        in_specs=[pl.BlockSpec((tm, d), lambda i: (i, 0))] + [_const_spec(a.shape) for a in ins[1:]],
        out_specs=[o[0] for o in outs],
        out_shape=[o[1] for o in outs],
        compiler_params=_params("arbitrary"),
        name="proj",
    )(*ins)


def _sb_weights(zs, carries, u2, masks):
    raws, sps = [], []
    for z, m in zip(zs, masks):
        raw = jnp.maximum(z, 0.0) + jnp.log(1.0 + jnp.exp(-jnp.abs(z)))
        raws.append(raw)
        sps.append(raw if m is None else jnp.where(m, raw, 0.0))
    cums = [_later_sum(sp, u2) for sp in sps]
    ws, new_carries = [], []
    for z, raw, sp, cum, carry, m in zip(zs, raws, sps, cums, carries, masks):
        w = jnp.exp((z - raw) - cum - carry)
        ws.append(w if m is None else jnp.where(m, w, 0.0))
        new_carries.append(carry + jnp.sum(sp, axis=-1, keepdims=True))
    return ws, new_carries


def _attn_prompt_kernel(q_ref, k_ref, v_ref, o_ref, *, tq, tk, n_heads, hd):
    qi = pl.program_id(1)
    page = k_ref.shape[-1]
    ppb = tk // page
    u2 = _later_sum_matrix(tk)
    heads = [slice(h * hd, (h + 1) * hd) for h in range(n_heads)]
    q = q_ref[...]
    qs = [q[:, sl].astype(BF16) for sl in heads]

    def block(jb, carries, accs, masks):
        def kv(ref, sl):
            return jnp.concatenate([ref[jb * ppb + i, sl, :] for i in range(ppb)], axis=-1)
        zs = [_dot(qs[h], kv(k_ref, heads[h])) for h in range(n_heads)]
        ws, carries = _sb_weights(zs, carries, u2, masks)
        accs = [acc + _dot_nt(w.astype(BF16), kv(v_ref, heads[h]))
                for h, (acc, w) in enumerate(zip(accs, ws))]
        return carries, accs

    jd = (qi * tq) // tk
    offset = qi * tq - jd * tk
    diag_mask = _iota((tq, tk), 1) < _iota((tq, tk), 0) + offset
    carries = [jnp.zeros((tq, 1), F32)] * n_heads
    accs = [jnp.zeros((tq, hd), F32)] * n_heads
    carries, accs = block(jd, carries, accs, [diag_mask] * n_heads)

    def body(i, ca):
        carries, accs = block(jd - 1 - i, list(ca[0]), list(ca[1]), [None] * n_heads)
        return tuple(carries), tuple(accs)

    carries, accs = lax.fori_loop(0, jd, body, (tuple(carries), tuple(accs)))
    outs = [acc * lax.rsqrt(jnp.mean(acc * acc, axis=-1, keepdims=True) + RMS_EPS) for acc in accs]
    o_ref[...] = jnp.concatenate(outs, axis=-1)


def _attn_prompt(q, kpages, vpages, *, bsz, t, hd):
    n, d_attn = q.shape
    n_pages, _, page = kpages.shape
    assert n == bsz * t and n_pages * page == n
    tq = _pick_tile(t, (256, 128, 64))
    tk = _pick_tile(t, (256, 128))
    assert tk % tq == 0 and tk % page == 0
    nq = t // tq
    ppb = t // page
    return pl.pallas_call(
        functools.partial(_attn_prompt_kernel, tq=tq, tk=tk, n_heads=d_attn // hd, hd=hd),
        grid=(bsz, nq),
        in_specs=[pl.BlockSpec((tq, d_attn), lambda b, i: (b * nq + i, 0)),
                  pl.BlockSpec((ppb, d_attn, page), lambda b, i: (b, 0, 0)),
                  pl.BlockSpec((ppb, d_attn, page), lambda b, i: (b, 0, 0))],
        out_specs=pl.BlockSpec((tq, d_attn), lambda b, i: (b * nq + i, 0)),
        out_shape=jax.ShapeDtypeStruct((n, d_attn), F32),
        compiler_params=_params("arbitrary", "arbitrary"),
        name="attn_prompt",
    )(q, kpages, vpages)


def _attn_sample_kernel(pt_ref, q_ref, kn_ref, vn_ref, *rest, n_pages, page, n_heads, hd, ts):
    del pt_ref
    kp_refs = rest[:n_pages]
    vp_refs = rest[n_pages:2 * n_pages]
    o_ref = rest[2 * n_pages]
    rows = n_heads * ts
    d_attn = n_heads * hd
    ts_shift, hd_shift = _log2(ts), _log2(hd)

    head_mask = (_iota((rows, d_attn), 0) >> ts_shift) == (_iota((rows, d_attn), 1) >> hd_shift)
    q = q_ref[0]
    qbd = jnp.where(head_mask, jnp.concatenate([q] * n_heads, axis=0), 0.0).astype(BF16)
    u2 = _later_sum_matrix(page)

    pad = jnp.zeros((page - ts, d_attn), F32)
    k_new = jnp.concatenate([kn_ref[0], pad], axis=0).astype(BF16)
    v_new = jnp.concatenate([vn_ref[0], pad], axis=0).astype(BF16)
    new_mask = _iota((rows, page), 1) < (_iota((rows, page), 0) & (ts - 1))
    zero = jnp.zeros((rows, 1), F32)
    (w,), (carry,) = _sb_weights([_dot_nt(qbd, k_new)], [zero], u2, [new_mask])
    acc = _dot(w.astype(BF16), v_new)
    order = list(reversed(range(n_pages)))
    zs = [_dot(qbd, kp_refs[p][0, 0].astype(BF16)) for p in order]
    raws = [jnp.maximum(z, 0.0) + jnp.log(1.0 + jnp.exp(-jnp.abs(z))) for z in zs]
    cums = [_later_sum(raw, u2) for raw in raws]
    for z, raw, cum, p in zip(zs, raws, cums, order):
        w = jnp.exp((z - raw) - cum - carry)
        carry = carry + jnp.sum(raw, axis=-1, keepdims=True)
        acc = acc + _dot_nt(w.astype(BF16), vp_refs[p][0, 0].astype(BF16))

    y = jnp.sum(jnp.where(head_mask, acc, 0.0).reshape(n_heads, ts, d_attn), axis=0)
    sq = y * y
    col_head = _iota((ts, d_attn), 1) >> hd_shift
    scale = jnp.zeros((ts, d_attn), F32)
    for h in range(n_heads):
        sel = col_head == h
        ms = jnp.sum(jnp.where(sel, sq, 0.0), axis=-1, keepdims=True) * (1.0 / hd)
        scale = jnp.where(sel, lax.rsqrt(ms + RMS_EPS), scale)
    o_ref[0] = y * scale


def _attn_sample(q, k_new, v_new, cache_k, cache_v, page_table_flat, layer, *, n_heads, hd):
    bsz, ts, d_attn = q.shape
    n_pages = page_table_flat.shape[0] // bsz
    page = cache_k.shape[3]
    assert ts <= page and ts % SUBLANES == 0

    def page_spec(p):
        return pl.BlockSpec((1, 1, d_attn, page), lambda b, pt: (layer, pt[b * n_pages + p], 0, 0))

    tok_spec = pl.BlockSpec((1, ts, d_attn), lambda b, pt: (b, 0, 0))
    grid_spec = pltpu.PrefetchScalarGridSpec(
        num_scalar_prefetch=1,
        grid=(bsz,),
        in_specs=[tok_spec, tok_spec, tok_spec]
        + [page_spec(p) for p in range(n_pages)] + [page_spec(p) for p in range(n_pages)],
        out_specs=tok_spec,
    )
    return pl.pallas_call(
        functools.partial(_attn_sample_kernel, n_pages=n_pages, page=page, n_heads=n_heads, hd=hd, ts=ts),
        grid_spec=grid_spec,
        out_shape=jax.ShapeDtypeStruct((bsz, ts, d_attn), F32),
        compiler_params=_params("arbitrary"),
        name="attn_sample",
    )(page_table_flat, q, k_new, v_new, *([cache_k] * n_pages), *([cache_v] * n_pages))


def _gelu_tanh(x):
    return 0.5 * x * (1.0 + jnp.tanh(math.sqrt(2.0 / math.pi) * (x + 0.044715 * (x * x * x))))


def _s5_kernel(u_ref, h0re_ref, h0im_ref, lamre_ref, lamim_ref, logdt_ref, braw_re_ref, braw_im_ref,
               cre_ref, cim_ref, d_ref, wglu_ref, bglu_ref,
               y_ref, hre_out_ref, him_out_ref,
               bu_re, bu_im, hs_re, hs_im, a_re_s, a_im_s, bbd_re, bbd_im,
               *, nb, tt, n_steps):
    step = pl.program_id(0)

    @pl.when(step == 0)
    def _():
        lam_re = lamre_ref[...]
        lam_im = lamim_ref[...]
        dt = jnp.exp(logdt_ref[...])
        mag = jnp.exp(lam_re * dt)
        a_re = mag * jnp.cos(lam_im * dt)
        a_im = mag * jnp.sin(lam_im * dt)
        den = lam_re * lam_re + lam_im * lam_im
        z_re = ((a_re - 1.0) * lam_re + a_im * lam_im) / den
        z_im = (a_im * lam_re - (a_re - 1.0) * lam_im) / den
        a_re_s[...] = a_re
        a_im_s[...] = a_im
        bbd_re[...] = (z_re * braw_re_ref[...] - z_im * braw_im_ref[...]).astype(BF16)
        bbd_im[...] = (z_re * braw_im_ref[...] + z_im * braw_re_ref[...]).astype(BF16)
        hs_re[...] = h0re_ref[...]
        hs_im[...] = h0im_ref[...]

    d_ssm = u_ref.shape[-1]
    u = u_ref[...]
    ub = u.astype(BF16)
    n_lb = bu_re.shape[0]
    lanes = [slice(j * LANES, (j + 1) * LANES) for j in range(n_lb)]
    for buf, bbd in ((bu_re, bbd_re), (bu_im, bbd_im)):
        full = _dot(ub, bbd[...])
        for j in range(n_lb):
            buf[j] = full[:, lanes[j]]

    a_re = [a_re_s[:, lanes[j]] for j in range(n_lb)]
    a_im = [a_im_s[:, lanes[j]] for j in range(n_lb)]
    for b0 in range(0, nb, SUBLANES):
        gs = min(SUBLANES, nb - b0)

        def scan_step(t, carry):
            start = t * nb + b0
            rows = pl.ds(pl.multiple_of(start, SUBLANES) if gs == SUBLANES and nb % SUBLANES == 0 else start, gs)
            out = []
            for j in range(n_lb):
                h_re, h_im = carry[2 * j], carry[2 * j + 1]
                n_re = a_re[j] * h_re - a_im[j] * h_im + bu_re[j, rows, :]
                n_im = a_re[j] * h_im + a_im[j] * h_re + bu_im[j, rows, :]
                bu_re[j, rows, :] = n_re
                bu_im[j, rows, :] = n_im
                out += [n_re, n_im]
            return tuple(out)

        init = []
        for j in range(n_lb):
            init += [hs_re[b0:b0 + gs, lanes[j]], hs_im[b0:b0 + gs, lanes[j]]]
        fin = lax.fori_loop(0, tt, scan_step, tuple(init), unroll=min(tt, 8))
        for j in range(n_lb):
            hs_re[b0:b0 + gs, lanes[j]] = fin[2 * j]
            hs_im[b0:b0 + gs, lanes[j]] = fin[2 * j + 1]

    states = lambda buf: jnp.concatenate([buf[j] for j in range(n_lb)], axis=-1).astype(BF16)
    y = _dot(states(bu_re), cre_ref[...]) - _dot(states(bu_im), cim_ref[...]) + d_ref[...] * u
    g = _gelu_tanh(y)
    ys = g * jax.nn.sigmoid(_dot(g.astype(BF16), wglu_ref[...]) + bglu_ref[...])
    ys = ys * lax.rsqrt(jnp.mean(ys * ys, axis=-1, keepdims=True) + RMS_EPS)
    y_ref[...] = ys

    @pl.when(step == n_steps - 1)
    def _():
        hre_out_ref[...] = hs_re[...]
        him_out_ref[...] = hs_im[...]


def _s5(u_tm, h0_re, h0_im, lw, *, t):
    n, d_ssm = u_tm.shape
    nb = n // t
    n_state = h0_re.shape[-1]
    tt = _pick_tile(t, (512, 256, 128, 64, 8))
    n_steps = t // tt
    consts = (lw["lam_re"], lw["lam_im"], lw["log_dt"], lw["braw_re"], lw["braw_im"],
              lw["c_re_bd"], lw["c_im_bd"], lw["d_skip"], lw["w_glu"], lw["b_glu"])
    u_spec = pl.BlockSpec((tt * nb, d_ssm), lambda i: (i, 0))
    return pl.pallas_call(
        functools.partial(_s5_kernel, nb=nb, tt=tt, n_steps=n_steps),
        grid=(n_steps,),
        in_specs=[u_spec, _const_spec(h0_re.shape), _const_spec(h0_im.shape)]
        + [_const_spec(c.shape) for c in consts],
        out_specs=[u_spec, _const_spec(h0_re.shape), _const_spec(h0_im.shape)],
        out_shape=[jax.ShapeDtypeStruct(u_tm.shape, F32),
                   jax.ShapeDtypeStruct(h0_re.shape, F32), jax.ShapeDtypeStruct(h0_im.shape, F32)],
        scratch_shapes=[pltpu.VMEM((n_state // LANES, nb * tt, LANES), F32),
                        pltpu.VMEM((n_state // LANES, nb * tt, LANES), F32),
                        pltpu.VMEM((nb, n_state), F32), pltpu.VMEM((nb, n_state), F32),
                        pltpu.VMEM((1, n_state), F32), pltpu.VMEM((1, n_state), F32),
                        pltpu.VMEM((d_ssm, n_state), BF16), pltpu.VMEM((d_ssm, n_state), BF16)],
        compiler_params=_params("arbitrary"),
        name="s5",
    )(u_tm, h0_re, h0_im, *consts)


def _mlstm_kernel(q_ref, k_ref, v_ref, o_ref, g_ref, c0_ref, n0_ref, m0_ref,
                  y_ref, c1_ref, n1_ref, m1_ref, c_s, n_s, m_s,
                  *, chunk, n_heads, hd, n_chunks):
    ci = pl.program_id(1)

    @pl.when(ci == 0)
    def _():
        c_s[...] = c0_ref[0]
        n_s[...] = n0_ref[0]
        m_s[...] = m0_ref[0]

    L = chunk
    lane = _iota((L, LANES), 1)
    tril = _iota((L, L), 0) >= _iota((L, L), 1)
    tril_b = tril.astype(BF16)
    triu_b = (_iota((L, L), 0) <= _iota((L, L), 1)).astype(BF16)
    pick = (_iota((SUBLANES, LANES), 0) == _iota((SUBLANES, LANES), 1)).astype(BF16)
    k_scale = hd ** -0.5

    gates = g_ref[0]
    is_f = (lane >= n_heads) & (lane < 2 * n_heads)
    gl = jnp.where(is_f, jax.nn.log_sigmoid(gates), gates)
    parts = _split3(gl)
    cum_col = sum(_dot(tril_b, p) for p in parts)
    g_row = sum(_dot_nt(pick, p) for p in parts)
    cum_row = sum(_dot(p, triu_b) for p in _split3(g_row))

    outs = []
    for h in range(n_heads):
        sl = slice(h * hd, (h + 1) * hd)
        q = q_ref[0, :, sl]
        k = k_ref[0, :, sl] * k_scale
        v = v_ref[0, :, sl]
        qb, kb = q.astype(BF16), k.astype(BF16)
        c_prev = c_s[h]
        n_prev = n_s[h]
        m_prev = m_s[h]

        i_col = gl[:, h:h + 1]
        b_col = cum_col[:, n_heads + h:n_heads + h + 1]
        i_row = g_row[h:h + 1, :]
        b_row = cum_row[n_heads + h:n_heads + h + 1, :]

        log_d = jnp.where(tril, b_col - b_row + i_row, -jnp.inf)
        log_inter = b_col + m_prev
        m_t = jnp.maximum(log_inter, jnp.max(log_d, axis=-1, keepdims=True))
        s = _dot_nt(qb, kb) * jnp.exp(log_d - m_t)
        w_inter = jnp.exp(log_inter - m_t)
        num = _dot(s.astype(BF16), v.astype(BF16)) + w_inter * _dot_nt(qb, c_prev.astype(BF16))
        den = jnp.sum(s, axis=-1, keepdims=True) + w_inter * jnp.sum(q * n_prev, axis=-1, keepdims=True)
        h_t = num / jnp.maximum(jnp.abs(den), jnp.exp(-m_t))

        m_new = m_t[L - 1:L, :]
        b_last = b_col[L - 1:L, :]
        w_s = jnp.exp(b_last - b_col + i_col - m_new)
        decay = jnp.exp(b_last + m_prev - m_new)
        c_s[h] = decay * c_prev + _dot_tn((w_s * v).astype(BF16), kb)
        n_s[h] = decay * n_prev + jnp.sum(w_s * k, axis=0, keepdims=True)
        m_s[h] = m_new

        y = h_t * jax.nn.sigmoid(o_ref[0, :, sl])
        outs.append(y * lax.rsqrt(jnp.mean(y * y, axis=-1, keepdims=True) + RMS_EPS))
    y_ref[0] = jnp.concatenate(outs, axis=-1)

    @pl.when(ci == n_chunks - 1)
    def _():
        c1_ref[0] = c_s[...]
        n1_ref[0] = n_s[...]
        m1_ref[0] = m_s[...]


def _mlstm(q, k, v, o, gates, c0, n0, m0, *, hd):
    bsz, t, d_ml = q.shape
    n_heads = d_ml // hd
    chunk = _pick_tile(t, (256, 128, 64, 8))
    n_chunks = t // chunk
    tok = lambda w: pl.BlockSpec((1, chunk, w), lambda b, c: (b, c, 0))
    st = lambda shape: pl.BlockSpec((1,) + shape[1:], lambda b, c: (b,) + (0,) * (len(shape) - 1))
    return pl.pallas_call(
        functools.partial(_mlstm_kernel, chunk=chunk, n_heads=n_heads, hd=hd, n_chunks=n_chunks),
        grid=(bsz, n_chunks),
        in_specs=[tok(d_ml), tok(d_ml), tok(d_ml), tok(d_ml), tok(LANES),
                  st(c0.shape), st(n0.shape), st(m0.shape)],
        out_specs=[tok(d_ml), st(c0.shape), st(n0.shape), st(m0.shape)],
        out_shape=[jax.ShapeDtypeStruct(q.shape, F32), jax.ShapeDtypeStruct(c0.shape, F32),
                   jax.ShapeDtypeStruct(n0.shape, F32), jax.ShapeDtypeStruct(m0.shape, F32)],
        scratch_shapes=[pltpu.VMEM(c0.shape[1:], F32), pltpu.VMEM(n0.shape[1:], F32), pltpu.VMEM(m0.shape[1:], F32)],
        compiler_params=_params("arbitrary", "arbitrary"),
        name="mlstm",
    )(q, k, v, o, gates, c0, n0, m0)


def _mlstm_seq_kernel(x_ref, g_ref, c0_ref, n0_ref, m0_ref, y_ref, c1_ref, n1_ref, m1_ref, hnum_s, *, ts, hd):
    k_scale = hd ** -0.5
    q = [x_ref[t, 0, 0] for t in range(ts)]
    k = [x_ref[t, 1, 0] * k_scale for t in range(ts)]
    m = m0_ref[0]
    n = n0_ref[0]
    fp, ip, den = [], [], []
    for t in range(ts):
        i_t = g_ref[0, t, 0:1, :]
        lf = jax.nn.log_sigmoid(g_ref[0, t, 1:2, :])
        m_new = jnp.maximum(lf + m, i_t)
        fp.append(jnp.exp(lf + m - m_new))
        ip.append(jnp.exp(i_t - m_new))
        m = m_new
        n = fp[t] * n + ip[t] * k[t]
        den.append(jnp.maximum(jnp.abs(jnp.sum(n * q[t], axis=0, keepdims=True)), jnp.exp(-m)))
    n1_ref[0] = n
    m1_ref[0] = m

    def rows(vb, carry):
        v0 = pl.multiple_of(vb * SUBLANES, SUBLANES)
        hn = [[None] * SUBLANES for _ in range(ts)]
        for r in range(SUBLANES):
            c = c0_ref[0, v0 + r]
            for t in range(ts):
                c = fp[t] * c + (ip[t] * x_ref[t, 2, 0, pl.ds(v0 + r, 1), :]) * k[t]
                hn[t][r] = jnp.sum(c * q[t], axis=0, keepdims=True)
            c1_ref[0, v0 + r] = c
        for t in range(ts):
            hnum_s[t, pl.ds(v0, SUBLANES), :] = jnp.concatenate(hn[t], axis=0)
        return carry

    lax.fori_loop(0, hd // SUBLANES, rows, 0)
    for t in range(ts):
        y = hnum_s[t] / den[t] * jax.nn.sigmoid(x_ref[t, 3, 0])
        y_ref[t, 0] = y * lax.rsqrt(jnp.mean(y * y, axis=0, keepdims=True) + RMS_EPS)


def _mlstm_seq(x, gates, c_all, layer, n0, m0, *, hd):
    ts, _, n_heads, _, nb = x.shape
    c0 = jax.ShapeDtypeStruct(c_all.shape[1:], F32)
    head = lambda shape: pl.BlockSpec((1,) + shape[1:], lambda h: (h,) + (0,) * (len(shape) - 1))
    y_spec = pl.BlockSpec((ts, 1, hd, nb), lambda h: (0, h, 0, 0))
    return pl.pallas_call(
        functools.partial(_mlstm_seq_kernel, ts=ts, hd=hd),
        grid=(n_heads,),
        in_specs=[pl.BlockSpec((ts, 4, 1, hd, nb), lambda h: (0, 0, h, 0, 0)), head(gates.shape),
                  pl.BlockSpec((None, 1, hd, hd, nb), lambda h: (layer, h, 0, 0, 0)),
                  head(n0.shape), head(m0.shape)],
        out_specs=[y_spec, head(c0.shape), head(n0.shape), head(m0.shape)],
        out_shape=[jax.ShapeDtypeStruct((ts, n_heads, hd, nb), F32), jax.ShapeDtypeStruct(c0.shape, F32),
                   jax.ShapeDtypeStruct(n0.shape, F32), jax.ShapeDtypeStruct(m0.shape, F32)],
        scratch_shapes=[pltpu.VMEM((ts, hd, nb), F32)],
        compiler_params=_params("arbitrary"),
        name="mlstm_seq",
    )(x, gates, c_all, n0, m0)


def _proj_t_kernel(x_ref, w_ref, b_ref, wg_ref, bg_ref, o_ref, g_ref):
    xb = x_ref[...].astype(BF16)
    o_ref[0] = _dot_nt(w_ref[...], xb) + b_ref[...]
    g_ref[0] = _dot_nt(wg_ref[...], xb) + bg_ref[...]


def _proj_t(x_tm, lw, *, ts):
    n, d = x_tm.shape
    nb = n // ts
    w, b, wg, bg = lw["w_ml_t"], lw["b_ml_t"], lw["w_gate_t"], lw["b_gate_t"]
    return pl.pallas_call(
        _proj_t_kernel,
        grid=(ts,),
        in_specs=[pl.BlockSpec((nb, d), lambda t: (t, 0))] + [_const_spec(a.shape) for a in (w, b, wg, bg)],
        out_specs=[pl.BlockSpec((1, w.shape[0], nb), lambda t: (t, 0, 0)),
                   pl.BlockSpec((1, wg.shape[0], nb), lambda t: (t, 0, 0))],
        out_shape=[jax.ShapeDtypeStruct((ts, w.shape[0], nb), F32), jax.ShapeDtypeStruct((ts, wg.shape[0], nb), F32)],
        compiler_params=_params("arbitrary"),
        name="proj_t",
    )(x_tm, w, b, wg, bg)


def _layer_norm(x, g, b):
    mu = jnp.mean(x, axis=-1, keepdims=True)
    xc = x - mu
    var = jnp.mean(xc * xc, axis=-1, keepdims=True)
    return xc * lax.rsqrt(var + LN_EPS) * g + b


def _dense_kernel(x_ref, ya_ref, ys_ref, ym_ref, mixg_ref, wout_ref, ln1g_ref, ln1b_ref,
                  wgate_ref, wup_ref, wdown_ref, ln2g_ref, ln2b_ref, o_ref, *, alpha, ff_chunks):
    d_a = ya_ref.shape[-1]
    d_s = ys_ref.shape[-1]
    d_m = ym_ref.shape[-1]
    mixed = (_dot((ya_ref[...] * mixg_ref[:, :d_a]).astype(BF16), wout_ref[:d_a, :])
             + _dot((ys_ref[...] * mixg_ref[:, d_a:d_a + d_s]).astype(BF16), wout_ref[d_a:d_a + d_s, :])
             + _dot((ym_ref[...] * mixg_ref[:, d_a + d_s:]).astype(BF16), wout_ref[d_a + d_s:d_a + d_s + d_m, :]))
    x1 = _layer_norm(alpha * x_ref[...] + mixed, ln1g_ref[...], ln1b_ref[...])
    x1b = x1.astype(BF16)
    f = jnp.zeros_like(x1)
    c0 = 0
    for width in ff_chunks:
        gate = _dot(x1b, wgate_ref[:, c0:c0 + width])
        up = _dot(x1b, wup_ref[:, c0:c0 + width])
        hid = (gate * jax.nn.sigmoid(gate) * up).astype(BF16)
        f = f + _dot(hid, wdown_ref[c0:c0 + width, :])
        c0 += width
    o_ref[...] = _layer_norm(alpha * x1 + f, ln2g_ref[...], ln2b_ref[...])


def _dense(x, ya, ys, ym, lw, *, alpha, d_ssm, seq_len=None):
    n, d = x.shape
    tm = _pick_tile(n, (256, 128, 64))
    d_ff = lw["w_gate"].shape[1]
    ff_chunks = []
    left = d_ff
    while left > 0:
        ff_chunks.append(min(1024, left))
        left -= ff_chunks[-1]
    row = lambda w: pl.BlockSpec((tm, w), lambda i: (i, 0))
    if seq_len is None:
        ys_spec = row(d_ssm)
    else:
        assert seq_len % tm == 0
        tps = seq_len // tm
        ys_spec = pl.BlockSpec((tm, d_ssm), lambda i: (i % tps, i // tps))
    consts = (lw["mix_g"], lw["w_out"], lw["ln1_g"], lw["ln1_b"], lw["w_gate"], lw["w_up"], lw["w_down"],
              lw["ln2_g"], lw["ln2_b"])
    return pl.pallas_call(
        functools.partial(_dense_kernel, alpha=alpha, ff_chunks=tuple(ff_chunks)),
        grid=(n // tm,),
        in_specs=[row(d), row(ya.shape[1]), ys_spec, row(ym.shape[1])]
        + [_const_spec(c.shape) for c in consts],
        out_specs=row(d),
        out_shape=jax.ShapeDtypeStruct((n, d), F32),
        compiler_params=_params("arbitrary"),
        name="dense",
    )(x, ya, ys, ym, *consts)


def _block_diag(blocks):
    g, r, c = blocks.shape
    eye = jnp.eye(g, dtype=blocks.dtype)
    return (eye[:, None, :, None] * blocks[:, :, None, :]).reshape(g * r, g * c)


def kernel(x_prompt, x_sample, cache_k, cache_v, page_table, state_ssm_re, state_ssm_im, state_mlstm_c, state_mlstm_n, state_mlstm_m, w_in, b_in, mlstm_gate_b, mix_g, w_out, ln1_g, ln1_b, ssm_lambda_re, ssm_lambda_im, ssm_log_dt, ssm_b_re, ssm_b_im, ssm_c_re, ssm_c_im, ssm_d, ssm_w_glu, ssm_b_glu, w_gate, w_up, w_down, ln2_g, ln2_b):
    depth = w_in.shape[0]
    bp, tp, d_model = x_prompt.shape
    bs, ts, _ = x_sample.shape
    _, n_pool, page, h_a, hd_a = cache_k.shape
    d_attn = h_a * hd_a
    _, n_groups, n_p = ssm_lambda_re.shape
    cg = ssm_b_re.shape[-1]
    d_ssm = n_groups * cg
    n_state = n_groups * n_p
    _, _, h_m, hd_m, _ = state_mlstm_c.shape
    d_ml = h_m * hd_m
    d_main = 3 * d_attn + d_ssm + 4 * d_ml
    assert w_in.shape[2] == d_main + 2 * h_m and 2 * h_m <= SUBLANES
    q_scale = hd_a ** -0.5
    assert 2.0 ** round(math.log2(q_scale)) == q_scale
    alpha = (2 * depth) ** 0.25

    cache_k2 = jnp.transpose(cache_k, (0, 1, 3, 4, 2)).reshape(depth, n_pool, d_attn, page)
    cache_v2 = jnp.transpose(cache_v, (0, 1, 3, 4, 2)).reshape(depth, n_pool, d_attn, page)
    pt_flat = page_table.reshape(-1).astype(jnp.int32)
    c_state_t = jnp.transpose(state_mlstm_c, (0, 2, 3, 4, 1))
    row = lambda a: a.reshape(1, -1).astype(F32)

    xp = x_prompt.reshape(bp * tp, d_model)
    xs = x_sample.reshape(bs * ts, d_model)
    zero_state = jnp.zeros((bp, n_state), F32)
    zero_c = jnp.zeros((bp, h_m, hd_m, hd_m), F32)
    zero_n = jnp.zeros((bp, h_m, 1, hd_m), F32)
    zero_m = jnp.zeros((bp, h_m, 1, 1), F32)

    outs_p = [[] for _ in range(7)]
    outs_s = [[] for _ in range(7)]
    for l in range(depth):
        gate_pad = LANES - 2 * h_m
        lw = {
            "w_main": w_in[l, :, :d_main].astype(BF16),
            "b_main": row(b_in[l, :d_main]),
            "w_gate_in": jnp.pad(w_in[l, :, d_main:], ((0, 0), (0, gate_pad))).astype(BF16),
            "w_kvt": w_in[l, :, d_attn:3 * d_attn].T.astype(BF16),
            "b_kvt": b_in[l, d_attn:3 * d_attn].reshape(-1, 1).astype(F32),
            "b_gate_in": row(jnp.pad(b_in[l, d_main:] + mlstm_gate_b[l], (0, gate_pad))),
            "w_ml_t": w_in[l, :, d_main - 4 * d_ml:d_main].T.astype(BF16),
            "b_ml_t": b_in[l, d_main - 4 * d_ml:d_main].reshape(-1, 1).astype(F32),
            "w_gate_t": w_in[l, :, d_main:].T.astype(BF16),
            "b_gate_t": (b_in[l, d_main:] + mlstm_gate_b[l]).reshape(-1, 1).astype(F32),
            "lam_re": row(ssm_lambda_re[l]),
            "lam_im": row(ssm_lambda_im[l]),
            "log_dt": row(jnp.repeat(ssm_log_dt[l], n_p)),
            "braw_re": _block_diag(jnp.swapaxes(ssm_b_re[l], 1, 2)).astype(F32),
            "braw_im": _block_diag(jnp.swapaxes(ssm_b_im[l], 1, 2)).astype(F32),
            "c_re_bd": _block_diag(jnp.swapaxes(ssm_c_re[l], 1, 2)).astype(BF16),
            "c_im_bd": _block_diag(jnp.swapaxes(ssm_c_im[l], 1, 2)).astype(BF16),
            "d_skip": row(ssm_d[l]),
            "w_glu": ssm_w_glu[l].astype(BF16),
            "b_glu": row(ssm_b_glu[l]),
            "mix_g": row(mix_g[l]),
            "w_out": w_out[l].astype(BF16),
            "ln1_g": row(ln1_g[l]), "ln1_b": row(ln1_b[l]),
            "w_gate": w_gate[l].astype(BF16), "w_up": w_up[l].astype(BF16), "w_down": w_down[l].astype(BF16),
            "ln2_g": row(ln2_g[l]), "ln2_b": row(ln2_b[l]),
        }
        proj = functools.partial(_proj, lw=lw, d_attn=d_attn, d_ssm=d_ssm, d_ml=d_ml, q_scale=q_scale)

        q, k, v, kb, vb, us, qm, km, vm, om, gates = proj(xp, seq_len=tp, page=page)
        ya = _attn_prompt(q, kb, vb, bsz=bp, t=tp, hd=hd_a)
        ys, h_re, h_im = _s5(us.reshape(tp * bp, d_ssm), zero_state, zero_state, lw, t=tp)
        r3 = lambda a: a.reshape(bp, tp, -1)
        ym, c1, n1, m1 = _mlstm(r3(qm), r3(km), r3(vm), r3(om), r3(gates), zero_c, zero_n, zero_m, hd=hd_m)
        xp = _dense(xp, ya, ys.reshape(tp, bp * d_ssm), ym.reshape(bp * tp, d_ml), lw, alpha=alpha, d_ssm=d_ssm,
                    seq_len=tp)
        for lst, a in zip(outs_p, (k, v, h_re, h_im, c1, n1, m1)):
            lst.append(a)

        q, k, v, us = proj(xs)
        r3 = lambda a: a.reshape(bs, ts, -1)
        to_tm = lambda a: jnp.swapaxes(a.reshape(bs, ts, -1), 0, 1).reshape(ts * bs, -1)
        from_tm = lambda a: jnp.swapaxes(a.reshape(ts, bs, -1), 0, 1).reshape(bs * ts, -1)
        ya = _attn_sample(r3(q), r3(k), r3(v), cache_k2, cache_v2, pt_flat, l, n_heads=h_a, hd=hd_a)
        ys, h_re, h_im = _s5(to_tm(us), state_ssm_re[l].reshape(bs, n_state), state_ssm_im[l].reshape(bs, n_state),
                             lw, t=ts)
        x_t, g_t = _proj_t(to_tm(xs), lw, ts=ts)
        ym_t, c1, n1, m1 = _mlstm_seq(
            x_t.reshape(ts, 4, h_m, hd_m, bs),
            jnp.transpose(g_t.reshape(ts, 2, h_m, bs), (2, 0, 1, 3)),
            c_state_t, l, jnp.transpose(state_mlstm_n[l], (1, 2, 0)),
            state_mlstm_m[l].T.reshape(h_m, 1, bs), hd=hd_m)
        ym = jnp.transpose(ym_t.reshape(ts, d_ml, bs), (2, 0, 1)).reshape(bs * ts, d_ml)
        xs = _dense(xs, ya.reshape(bs * ts, d_attn), from_tm(ys), ym, lw, alpha=alpha, d_ssm=d_ssm)
        for lst, a in zip(outs_s, (k, v, h_re, h_im, c1, n1, m1)):
            lst.append(a)

    st = lambda lst: jnp.stack(lst)
    paged = lambda lst: jnp.transpose(st(lst).reshape(depth, bp, tp // page, h_a, hd_a, page), (0, 1, 2, 5, 3, 4))
    k_prompt = paged(outs_p[0])
    v_prompt = paged(outs_p[1])
    k_sample = st(outs_s[0]).reshape(depth, bs, ts, h_a, hd_a)
    v_sample = st(outs_s[1]).reshape(depth, bs, ts, h_a, hd_a)
    ssm = lambda lst, b: st(lst).reshape(depth, b, n_groups, n_p)
    return (xp.reshape(bp, tp, d_model), xs.reshape(bs, ts, d_model),
            k_prompt, v_prompt, k_sample, v_sample,
            ssm(outs_p[2], bp), ssm(outs_p[3], bp), ssm(outs_s[2], bs), ssm(outs_s[3], bs),
            st(outs_p[4]), st(outs_p[5]).reshape(depth, bp, h_m, hd_m), st(outs_p[6]).reshape(depth, bp, h_m),
            jnp.transpose(st(outs_s[4]), (0, 4, 1, 2, 3)), jnp.transpose(st(outs_s[5]), (0, 3, 1, 2)),
            jnp.transpose(st(outs_s[6]).reshape(depth, h_m, bs), (0, 2, 1)))
```

```python
import functools
import math

import jax
import jax.numpy as jnp
from jax import lax
from jax.experimental import pallas as pl
from jax.experimental.pallas import tpu as pltpu

F32 = jnp.float32
BF16 = jnp.bfloat16

LN_EPS = 1e-5
RMS_EPS = 1e-6
V7X_VMEM_LIMIT_BYTES = 56 * 1024 * 1024
LANES = 128
SUBLANES = 8


def _dot(a, b):
    return jnp.dot(a, b, preferred_element_type=F32)


def _dot_nt(a, b):
    return lax.dot_general(a, b, (((1,), (1,)), ((), ())), preferred_element_type=F32)


def _dot_tn(a, b):
    return lax.dot_general(a, b, (((0,), (0,)), ((), ())), preferred_element_type=F32)


def _split2(x):
    hi = x.astype(BF16)
    lo = (x - hi.astype(F32)).astype(BF16)
    return hi, lo


def _tail_sum_matrix(k):
    return ((_iota((2 * k, k), 0) & (k - 1)) >= _iota((2 * k, k), 1)).astype(BF16)


def _tail_sum(x, u2):
    return _dot(jnp.concatenate(_split2(x), axis=-1), u2)


def _split3(x):
    hi = x.astype(BF16)
    r = x - hi.astype(F32)
    mid = r.astype(BF16)
    lo = (r - mid.astype(F32)).astype(BF16)
    return hi, mid, lo


def _iota(shape, dim):
    return lax.broadcasted_iota(jnp.int32, shape, dim)


def _log2(n):
    k = int(math.log2(n))
    assert 1 << k == n, n
    return k


def _pick_tile(n, candidates):
    for c in candidates:
        if n % c == 0:
            return c
    raise ValueError(f"no tile for {n} in {candidates}")


def _params(*sem):
    return pltpu.CompilerParams(dimension_semantics=sem, vmem_limit_bytes=V7X_VMEM_LIMIT_BYTES)


def _const_spec(shape):
    nd = len(shape)
    return pl.BlockSpec(shape, lambda *_: (0,) * nd)


def _layer_spec(stacked, layer):
    nd = stacked.ndim - 1
    return pl.BlockSpec((None,) + stacked.shape[1:], lambda *_: (layer,) + (0,) * nd)


def _proj_kernel(x_ref, w_ref, b_ref, wg_ref, bg_ref, *rest, d_attn, d_ssm, d_ml, q_scale, paged):
    xb = x_ref[...].astype(BF16)

    def cols(c0, width):
        return _dot(xb, w_ref[:, c0:c0 + width]) + b_ref[:, c0:c0 + width]

    if paged:
        wkvt_ref, bkvt_ref, _, _, q_ref, kp_ref, vp_ref, kpb_ref, vpb_ref = rest[:9]
        kvt = _dot_nt(wkvt_ref[...], xb) + bkvt_ref[...]
        page = kp_ref.shape[-1]
        for j in range(kp_ref.shape[0]):
            kt = kvt[:d_attn, j * page:(j + 1) * page]
            vt = kvt[d_attn:, j * page:(j + 1) * page]
            kp_ref[j] = kt
            vp_ref[j] = vt
            kpb_ref[j] = kt.astype(BF16)
            vpb_ref[j] = vt.astype(BF16)
        rest = rest[9:]
    else:
        q_ref, k_ref, v_ref = rest[:3]
        k_ref[...] = cols(d_attn, d_attn)
        v_ref[...] = cols(2 * d_attn, d_attn)
        rest = rest[3:]
    q_ref[...] = cols(0, d_attn) * q_scale
    c = 3 * d_attn
    rest[0][...] = cols(c, d_ssm)
    if len(rest) > 1:
        qm_ref, km_ref, vm_ref, om_ref, g_ref = rest[1:]
        c += d_ssm
        for ref in (qm_ref, km_ref, vm_ref, om_ref):
            ref[...] = cols(c, d_ml)
            c += d_ml
        g_ref[...] = _dot(xb, wg_ref[...]) + bg_ref[...]


def _proj(x, lw, layer, *, d_attn, d_ssm, d_ml, q_scale, seq_len=None, page=None, kv_all=None):
    n, d = x.shape
    tm = _pick_tile(n, (512, 256, 128, 64))
    row = lambda w, dt=F32: (pl.BlockSpec((tm, w), lambda i: (i, 0)), jax.ShapeDtypeStruct((n, w), dt))
    ins = [x, lw["w_main"], lw["b_main"], lw["w_gate_in"], lw["b_gate_in"]]
    in_specs = [pl.BlockSpec((tm, d), lambda i: (i, 0))] + [_layer_spec(a, layer) for a in ins[1:]]
    aliases = {}
    if page is None:
        outs = [row(d_attn), row(d_attn), row(d_attn), row(d_ssm)]
    else:
        assert tm % page == 0 and seq_len % tm == 0
        ins += [lw["w_kvt"], lw["b_kvt"], *kv_all]
        in_specs += [_layer_spec(lw["w_kvt"], layer), _layer_spec(lw["b_kvt"], layer),
                     pl.BlockSpec(memory_space=pl.ANY), pl.BlockSpec(memory_space=pl.ANY)]
        aliases = {len(ins) - 2: 1, len(ins) - 1: 2}
        slab_all = (pl.BlockSpec((None, tm // page, d_attn, page), lambda i: (layer, i, 0, 0)),
                    jax.ShapeDtypeStruct(kv_all[0].shape, F32))
        slab = lambda dt: (pl.BlockSpec((tm // page, d_attn, page), lambda i: (i, 0, 0)),
                           jax.ShapeDtypeStruct((n // page, d_attn, page), dt))
        tps = seq_len // tm
        us_tm = (pl.BlockSpec((tm, d_ssm), lambda i: (i % tps, i // tps)),
                 jax.ShapeDtypeStruct((seq_len, n // seq_len * d_ssm), F32))
        outs = [row(d_attn), slab_all, slab_all, slab(BF16), slab(BF16), us_tm,
                row(d_ml), row(d_ml), row(d_ml), row(d_ml), row(LANES)]
    return pl.pallas_call(
        functools.partial(_proj_kernel, d_attn=d_attn, d_ssm=d_ssm, d_ml=d_ml, q_scale=q_scale,
                          paged=page is not None),
        grid=(n // tm,),
        in_specs=in_specs,
        out_specs=[o[0] for o in outs],
        out_shape=[o[1] for o in outs],
        input_output_aliases=aliases,
        compiler_params=_params("arbitrary"),
        name="proj",
    )(*ins)


def _sb_weights(zs, carries, u2, masks):
    sps = []
    for z, m in zip(zs, masks):
        sp = jnp.maximum(z, 0.0) + jnp.log(1.0 + jnp.exp(-jnp.abs(z)))
        sps.append(sp if m is None else jnp.where(m, sp, 0.0))
    tails = [_tail_sum(sp, u2) for sp in sps]
    ws, new_carries = [], []
    for z, sp, tail, carry, m in zip(zs, sps, tails, carries, masks):
        w = jnp.exp(z - tail - carry)
        ws.append(w if m is None else jnp.where(m, w, 0.0))
        new_carries.append(carry + jnp.sum(sp, axis=-1, keepdims=True))
    return ws, new_carries


def _attn_prompt_kernel(q_ref, k_ref, v_ref, o_ref, *, tq, tk, n_heads, hd):
    qi = pl.program_id(1)
    page = k_ref.shape[-1]
    ppb = tk // page
    u2 = _tail_sum_matrix(tk)
    heads = [slice(h * hd, (h + 1) * hd) for h in range(n_heads)]
    q = q_ref[...]
    qs = [q[:, sl].astype(BF16) for sl in heads]

    def block(jb, carries, accs, masks):
        def kv(ref, sl):
            return jnp.concatenate([ref[jb * ppb + i, sl, :] for i in range(ppb)], axis=-1)
        zs = [_dot(qs[h], kv(k_ref, heads[h])) for h in range(n_heads)]
        ws, carries = _sb_weights(zs, carries, u2, masks)
        accs = [acc + _dot_nt(w.astype(BF16), kv(v_ref, heads[h]))
                for h, (acc, w) in enumerate(zip(accs, ws))]
        return carries, accs

    jd = (qi * tq) // tk
    offset = qi * tq - jd * tk
    diag_mask = _iota((tq, tk), 1) < _iota((tq, tk), 0) + offset
    carries = [jnp.zeros((tq, 1), F32)] * n_heads
    accs = [jnp.zeros((tq, hd), F32)] * n_heads
    carries, accs = block(jd, carries, accs, [diag_mask] * n_heads)

    def body(i, ca):
        carries, accs = block(jd - 1 - i, list(ca[0]), list(ca[1]), [None] * n_heads)
        return tuple(carries), tuple(accs)

    carries, accs = lax.fori_loop(0, jd, body, (tuple(carries), tuple(accs)))
    outs = [acc * lax.rsqrt(jnp.mean(acc * acc, axis=-1, keepdims=True) + RMS_EPS) for acc in accs]
    o_ref[...] = jnp.concatenate(outs, axis=-1)


def _attn_prompt(q, kpages, vpages, *, bsz, t, hd):
    n, d_attn = q.shape
    n_pages, _, page = kpages.shape
    assert n == bsz * t and n_pages * page == n
    tq = _pick_tile(t, (256, 128, 64))
    tk = _pick_tile(t, (256, 128))
    assert tk % tq == 0 and tk % page == 0
    nq = t // tq
    ppb = t // page
    return pl.pallas_call(
        functools.partial(_attn_prompt_kernel, tq=tq, tk=tk, n_heads=d_attn // hd, hd=hd),
        grid=(bsz, nq),
        in_specs=[pl.BlockSpec((tq, d_attn), lambda b, i: (b * nq + i, 0)),
                  pl.BlockSpec((ppb, d_attn, page), lambda b, i: (b, 0, 0)),
                  pl.BlockSpec((ppb, d_attn, page), lambda b, i: (b, 0, 0))],
        out_specs=pl.BlockSpec((tq, d_attn), lambda b, i: (b * nq + i, 0)),
        out_shape=jax.ShapeDtypeStruct((n, d_attn), F32),
        compiler_params=_params("arbitrary", "arbitrary"),
        name="attn_prompt",
    )(q, kpages, vpages)


def _attn_sample_kernel(pt_ref, q_ref, kn_ref, vn_ref, *rest, n_pages, page, n_heads, hd, ts):
    del pt_ref
    kp_refs = rest[:n_pages]
    vp_refs = rest[n_pages:2 * n_pages]
    o_ref = rest[2 * n_pages]
    rows = n_heads * ts
    d_attn = n_heads * hd
    ts_shift, hd_shift = _log2(ts), _log2(hd)

    head_mask = (_iota((rows, d_attn), 0) >> ts_shift) == (_iota((rows, d_attn), 1) >> hd_shift)
    q = q_ref[0]
    qbd = jnp.where(head_mask, jnp.concatenate([q] * n_heads, axis=0), 0.0).astype(BF16)
    u2 = _tail_sum_matrix(page)

    pad = jnp.zeros((page - ts, d_attn), F32)
    k_new = jnp.concatenate([kn_ref[0], pad], axis=0).astype(BF16)
    v_new = jnp.concatenate([vn_ref[0], pad], axis=0).astype(BF16)
    new_mask = _iota((rows, page), 1) < (_iota((rows, page), 0) & (ts - 1))
    zero = jnp.zeros((rows, 1), F32)
    (w,), (carry,) = _sb_weights([_dot_nt(qbd, k_new)], [zero], u2, [new_mask])
    acc = _dot(w.astype(BF16), v_new)
    order = list(reversed(range(n_pages)))
    zs = [_dot(qbd, kp_refs[p][0, 0].astype(BF16)) for p in order]
    sps = [jnp.maximum(z, 0.0) + jnp.log(1.0 + jnp.exp(-jnp.abs(z))) for z in zs]
    tails = [_tail_sum(sp, u2) for sp in sps]
    for z, sp, tail, p in zip(zs, sps, tails, order):
        w = jnp.exp(z - tail - carry)
        carry = carry + jnp.sum(sp, axis=-1, keepdims=True)
        acc = acc + _dot_nt(w.astype(BF16), vp_refs[p][0, 0].astype(BF16))

    y = jnp.sum(jnp.where(head_mask, acc, 0.0).reshape(n_heads, ts, d_attn), axis=0)
    sq = y * y
    col_head = _iota((ts, d_attn), 1) >> hd_shift
    scale = jnp.zeros((ts, d_attn), F32)
    for h in range(n_heads):
        sel = col_head == h
        ms = jnp.sum(jnp.where(sel, sq, 0.0), axis=-1, keepdims=True) * (1.0 / hd)
        scale = jnp.where(sel, lax.rsqrt(ms + RMS_EPS), scale)
    o_ref[0] = y * scale


def _attn_sample(q, k_new, v_new, cache_k, cache_v, page_table_flat, layer, *, n_heads, hd):
    bsz, ts, d_attn = q.shape
    n_pages = page_table_flat.shape[0] // bsz
    page = cache_k.shape[3]
    assert ts <= page and ts % SUBLANES == 0

    def page_spec(p):
        return pl.BlockSpec((1, 1, d_attn, page), lambda b, pt: (layer, pt[b * n_pages + p], 0, 0))

    tok_spec = pl.BlockSpec((1, ts, d_attn), lambda b, pt: (b, 0, 0))
    grid_spec = pltpu.PrefetchScalarGridSpec(
        num_scalar_prefetch=1,
        grid=(bsz,),
        in_specs=[tok_spec, tok_spec, tok_spec]
        + [page_spec(p) for p in range(n_pages)] + [page_spec(p) for p in range(n_pages)],
        out_specs=tok_spec,
    )
    return pl.pallas_call(
        functools.partial(_attn_sample_kernel, n_pages=n_pages, page=page, n_heads=n_heads, hd=hd, ts=ts),
        grid_spec=grid_spec,
        out_shape=jax.ShapeDtypeStruct((bsz, ts, d_attn), F32),
        compiler_params=_params("arbitrary"),
        name="attn_sample",
    )(page_table_flat, q, k_new, v_new, *([cache_k] * n_pages), *([cache_v] * n_pages))


def _gelu_tanh(x):
    return 0.5 * x * (1.0 + jnp.tanh(math.sqrt(2.0 / math.pi) * (x + 0.044715 * (x * x * x))))


def _s5_kernel(u_ref, h0re_ref, h0im_ref, lamre_ref, lamim_ref, logdt_ref, braw_re_ref, braw_im_ref,
               cre_ref, cim_ref, d_ref, wglu_ref, bglu_ref,
               y_ref, hre_out_ref, him_out_ref,
               bu_re, bu_im, hs_re, hs_im, a_re_s, a_im_s, bbd_re, bbd_im,
               *, nb, tt, n_steps):
    step = pl.program_id(0)

    @pl.when(step == 0)
    def _():
        lam_re = lamre_ref[...]
        lam_im = lamim_ref[...]
        dt = jnp.exp(logdt_ref[...])
        mag = jnp.exp(lam_re * dt)
        a_re = mag * jnp.cos(lam_im * dt)
        a_im = mag * jnp.sin(lam_im * dt)
        den = lam_re * lam_re + lam_im * lam_im
        z_re = ((a_re - 1.0) * lam_re + a_im * lam_im) / den
        z_im = (a_im * lam_re - (a_re - 1.0) * lam_im) / den
        a_re_s[...] = a_re
        a_im_s[...] = a_im
        bbd_re[...] = (z_re * braw_re_ref[...] - z_im * braw_im_ref[...]).astype(BF16)
        bbd_im[...] = (z_re * braw_im_ref[...] + z_im * braw_re_ref[...]).astype(BF16)
        hs_re[...] = h0re_ref[...]
        hs_im[...] = h0im_ref[...]

    d_ssm = u_ref.shape[-1]
    u = u_ref[...]
    ub = u.astype(BF16)
    n_lb = bu_re.shape[0]
    lanes = [slice(j * LANES, (j + 1) * LANES) for j in range(n_lb)]
    for buf, bbd in ((bu_re, bbd_re), (bu_im, bbd_im)):
        full = _dot(ub, bbd[...])
        for j in range(n_lb):
            buf[j] = full[:, lanes[j]]

    a_re = [a_re_s[:, lanes[j]] for j in range(n_lb)]
    a_im = [a_im_s[:, lanes[j]] for j in range(n_lb)]
    for b0 in range(0, nb, SUBLANES):
        gs = min(SUBLANES, nb - b0)

        def scan_step(t, carry):
            start = t * nb + b0
            rows = pl.ds(pl.multiple_of(start, SUBLANES) if gs == SUBLANES and nb % SUBLANES == 0 else start, gs)
            out = []
            for j in range(n_lb):
                h_re, h_im = carry[2 * j], carry[2 * j + 1]
                n_re = a_re[j] * h_re - a_im[j] * h_im + bu_re[j, rows, :]
                n_im = a_re[j] * h_im + a_im[j] * h_re + bu_im[j, rows, :]
                bu_re[j, rows, :] = n_re
                bu_im[j, rows, :] = n_im
                out += [n_re, n_im]
            return tuple(out)

        init = []
        for j in range(n_lb):
            init += [hs_re[b0:b0 + gs, lanes[j]], hs_im[b0:b0 + gs, lanes[j]]]
        fin = lax.fori_loop(0, tt, scan_step, tuple(init), unroll=min(tt, 8))
        for j in range(n_lb):
            hs_re[b0:b0 + gs, lanes[j]] = fin[2 * j]
            hs_im[b0:b0 + gs, lanes[j]] = fin[2 * j + 1]

    states = lambda buf: jnp.concatenate([buf[j] for j in range(n_lb)], axis=-1).astype(BF16)
    y = _dot(states(bu_re), cre_ref[...]) - _dot(states(bu_im), cim_ref[...]) + d_ref[...] * u
    g = _gelu_tanh(y)
    ys = g * jax.nn.sigmoid(_dot(g.astype(BF16), wglu_ref[...]) + bglu_ref[...])
    ys = ys * lax.rsqrt(jnp.mean(ys * ys, axis=-1, keepdims=True) + RMS_EPS)
    y_ref[...] = ys

    @pl.when(step == n_steps - 1)
    def _():
        hre_out_ref[...] = hs_re[...]
        him_out_ref[...] = hs_im[...]


def _s5(u_tm, h0_re, h0_im, lw, layer, *, t):
    n, d_ssm = u_tm.shape
    nb = n // t
    n_state = h0_re.shape[-1]
    tt = _pick_tile(t, (512, 256, 128, 64, 8))
    n_steps = t // tt
    consts = (lw["lam_re"], lw["lam_im"], lw["log_dt"], lw["braw_re"], lw["braw_im"],
              lw["c_re_bd"], lw["c_im_bd"], lw["d_skip"], lw["w_glu"], lw["b_glu"])
    u_spec = pl.BlockSpec((tt * nb, d_ssm), lambda i: (i, 0))
    return pl.pallas_call(
        functools.partial(_s5_kernel, nb=nb, tt=tt, n_steps=n_steps),
        grid=(n_steps,),
        in_specs=[u_spec, _const_spec(h0_re.shape), _const_spec(h0_im.shape)]
        + [_layer_spec(c, layer) for c in consts],
        out_specs=[u_spec, _const_spec(h0_re.shape), _const_spec(h0_im.shape)],
        out_shape=[jax.ShapeDtypeStruct(u_tm.shape, F32),
                   jax.ShapeDtypeStruct(h0_re.shape, F32), jax.ShapeDtypeStruct(h0_im.shape, F32)],
        scratch_shapes=[pltpu.VMEM((n_state // LANES, nb * tt, LANES), F32),
                        pltpu.VMEM((n_state // LANES, nb * tt, LANES), F32),
                        pltpu.VMEM((nb, n_state), F32), pltpu.VMEM((nb, n_state), F32),
                        pltpu.VMEM((1, n_state), F32), pltpu.VMEM((1, n_state), F32),
                        pltpu.VMEM((d_ssm, n_state), BF16), pltpu.VMEM((d_ssm, n_state), BF16)],
        compiler_params=_params("arbitrary"),
        name="s5",
    )(u_tm, h0_re, h0_im, *consts)


def _mlstm_kernel(q_ref, k_ref, v_ref, o_ref, g_ref, c0_ref, n0_ref, m0_ref,
                  y_ref, c1_ref, n1_ref, m1_ref, c_s, n_s, m_s,
                  *, chunk, n_heads, hd, n_chunks):
    ci = pl.program_id(1)

    @pl.when(ci == 0)
    def _():
        c_s[...] = c0_ref[0]
        n_s[...] = n0_ref[0]
        m_s[...] = m0_ref[0]

    L = chunk
    lane = _iota((L, LANES), 1)
    tril = _iota((L, L), 0) >= _iota((L, L), 1)
    tril_b = tril.astype(BF16)
    triu_b = (_iota((L, L), 0) <= _iota((L, L), 1)).astype(BF16)
    pick = (_iota((SUBLANES, LANES), 0) == _iota((SUBLANES, LANES), 1)).astype(BF16)
    k_scale = hd ** -0.5

    gates = g_ref[0]
    is_f = (lane >= n_heads) & (lane < 2 * n_heads)
    gl = jnp.where(is_f, jax.nn.log_sigmoid(gates), gates)
    parts = _split3(gl)
    cum_col = sum(_dot(tril_b, p) for p in parts)
    g_row = sum(_dot_nt(pick, p) for p in parts)
    cum_row = sum(_dot(p, triu_b) for p in _split3(g_row))

    outs = []
    for h in range(n_heads):
        sl = slice(h * hd, (h + 1) * hd)
        q = q_ref[0, :, sl]
        k = k_ref[0, :, sl] * k_scale
        v = v_ref[0, :, sl]
        qb, kb = q.astype(BF16), k.astype(BF16)
        c_prev = c_s[h]
        n_prev = n_s[h]
        m_prev = m_s[h]

        i_col = gl[:, h:h + 1]
        b_col = cum_col[:, n_heads + h:n_heads + h + 1]
        i_row = g_row[h:h + 1, :]
        b_row = cum_row[n_heads + h:n_heads + h + 1, :]

        log_d = jnp.where(tril, b_col - b_row + i_row, -jnp.inf)
        log_inter = b_col + m_prev
        m_t = jnp.maximum(log_inter, jnp.max(log_d, axis=-1, keepdims=True))
        s = _dot_nt(qb, kb) * jnp.exp(log_d - m_t)
        w_inter = jnp.exp(log_inter - m_t)
        num = _dot(s.astype(BF16), v.astype(BF16)) + w_inter * _dot_nt(qb, c_prev.astype(BF16))
        den = jnp.sum(s, axis=-1, keepdims=True) + w_inter * jnp.sum(q * n_prev, axis=-1, keepdims=True)
        h_t = num / jnp.maximum(jnp.abs(den), jnp.exp(-m_t))

        m_new = m_t[L - 1:L, :]
        b_last = b_col[L - 1:L, :]
        w_s = jnp.exp(b_last - b_col + i_col - m_new)
        decay = jnp.exp(b_last + m_prev - m_new)
        c_s[h] = decay * c_prev + _dot_tn((w_s * v).astype(BF16), kb)
        n_s[h] = decay * n_prev + jnp.sum(w_s * k, axis=0, keepdims=True)
        m_s[h] = m_new

        y = h_t * jax.nn.sigmoid(o_ref[0, :, sl])
        outs.append(y * lax.rsqrt(jnp.mean(y * y, axis=-1, keepdims=True) + RMS_EPS))
    y_ref[0] = jnp.concatenate(outs, axis=-1)

    @pl.when(ci == n_chunks - 1)
    def _():
        c1_ref[0] = c_s[...]
        n1_ref[0] = n_s[...]
        m1_ref[0] = m_s[...]


def _mlstm(q, k, v, o, gates, c0, n0, m0, *, hd):
    bsz, t, d_ml = q.shape
    n_heads = d_ml // hd
    chunk = _pick_tile(t, (256, 128, 64, 8))
    n_chunks = t // chunk
    tok = lambda w: pl.BlockSpec((1, chunk, w), lambda b, c: (b, c, 0))
    st = lambda shape: pl.BlockSpec((1,) + shape[1:], lambda b, c: (b,) + (0,) * (len(shape) - 1))
    return pl.pallas_call(
        functools.partial(_mlstm_kernel, chunk=chunk, n_heads=n_heads, hd=hd, n_chunks=n_chunks),
        grid=(bsz, n_chunks),
        in_specs=[tok(d_ml), tok(d_ml), tok(d_ml), tok(d_ml), tok(LANES),
                  st(c0.shape), st(n0.shape), st(m0.shape)],
        out_specs=[tok(d_ml), st(c0.shape), st(n0.shape), st(m0.shape)],
        out_shape=[jax.ShapeDtypeStruct(q.shape, F32), jax.ShapeDtypeStruct(c0.shape, F32),
                   jax.ShapeDtypeStruct(n0.shape, F32), jax.ShapeDtypeStruct(m0.shape, F32)],
        scratch_shapes=[pltpu.VMEM(c0.shape[1:], F32), pltpu.VMEM(n0.shape[1:], F32), pltpu.VMEM(m0.shape[1:], F32)],
        compiler_params=_params("arbitrary", "arbitrary"),
        name="mlstm",
    )(q, k, v, o, gates, c0, n0, m0)


def _mlstm_seq_kernel(x_ref, g_ref, c0_ref, n0_ref, m0_ref, y_ref, c1_ref, n1_ref, m1_ref, hnum_s, *, ts, hd):
    k_scale = hd ** -0.5
    q = [x_ref[t, 0, 0] for t in range(ts)]
    k = [x_ref[t, 1, 0] * k_scale for t in range(ts)]
    m = m0_ref[0]
    n = n0_ref[0]
    fp, ip, den = [], [], []
    for t in range(ts):
        i_t = g_ref[0, t, 0:1, :]
        lf = jax.nn.log_sigmoid(g_ref[0, t, 1:2, :])
        m_new = jnp.maximum(lf + m, i_t)
        fp.append(jnp.exp(lf + m - m_new))
        ip.append(jnp.exp(i_t - m_new))
        m = m_new
        n = fp[t] * n + ip[t] * k[t]
        den.append(jnp.maximum(jnp.abs(jnp.sum(n * q[t], axis=0, keepdims=True)), jnp.exp(-m)))
    n1_ref[0] = n
    m1_ref[0] = m

    def rows(vb, carry):
        v0 = pl.multiple_of(vb * SUBLANES, SUBLANES)
        hn = [[None] * SUBLANES for _ in range(ts)]
        for r in range(SUBLANES):
            c = c0_ref[0, v0 + r]
            for t in range(ts):
                c = fp[t] * c + (ip[t] * x_ref[t, 2, 0, pl.ds(v0 + r, 1), :]) * k[t]
                hn[t][r] = jnp.sum(c * q[t], axis=0, keepdims=True)
            c1_ref[0, v0 + r] = c
        for t in range(ts):
            hnum_s[t, pl.ds(v0, SUBLANES), :] = jnp.concatenate(hn[t], axis=0)
        return carry

    lax.fori_loop(0, hd // SUBLANES, rows, 0)
    for t in range(ts):
        y = hnum_s[t] / den[t] * jax.nn.sigmoid(x_ref[t, 3, 0])
        y_ref[t, 0] = y * lax.rsqrt(jnp.mean(y * y, axis=0, keepdims=True) + RMS_EPS)


def _mlstm_seq(x, gates, c_all, layer, n0, m0, *, hd):
    ts, _, n_heads, _, nb = x.shape
    c0 = jax.ShapeDtypeStruct(c_all.shape[1:], F32)
    head = lambda shape: pl.BlockSpec((1,) + shape[1:], lambda h: (h,) + (0,) * (len(shape) - 1))
    y_spec = pl.BlockSpec((ts, 1, hd, nb), lambda h: (0, h, 0, 0))
    return pl.pallas_call(
        functools.partial(_mlstm_seq_kernel, ts=ts, hd=hd),
        grid=(n_heads,),
        in_specs=[pl.BlockSpec((ts, 4, 1, hd, nb), lambda h: (0, 0, h, 0, 0)), head(gates.shape),
                  pl.BlockSpec((None, 1, hd, hd, nb), lambda h: (layer, h, 0, 0, 0)),
                  head(n0.shape), head(m0.shape)],
        out_specs=[y_spec, head(c0.shape), head(n0.shape), head(m0.shape)],
        out_shape=[jax.ShapeDtypeStruct((ts, n_heads, hd, nb), F32), jax.ShapeDtypeStruct(c0.shape, F32),
                   jax.ShapeDtypeStruct(n0.shape, F32), jax.ShapeDtypeStruct(m0.shape, F32)],
        scratch_shapes=[pltpu.VMEM((ts, hd, nb), F32)],
        compiler_params=_params("arbitrary"),
        name="mlstm_seq",
    )(x, gates, c_all, n0, m0)


def _proj_t_kernel(x_ref, w_ref, b_ref, wg_ref, bg_ref, o_ref, g_ref):
    xb = x_ref[...].astype(BF16)
    o_ref[0] = _dot_nt(w_ref[...], xb) + b_ref[...]
    g_ref[0] = _dot_nt(wg_ref[...], xb) + bg_ref[...]


def _proj_t(x_tm, lw, layer, *, ts):
    n, d = x_tm.shape
    nb = n // ts
    w, b, wg, bg = lw["w_ml_t"], lw["b_ml_t"], lw["w_gate_t"], lw["b_gate_t"]
    return pl.pallas_call(
        _proj_t_kernel,
        grid=(ts,),
        in_specs=[pl.BlockSpec((nb, d), lambda t: (t, 0))] + [_layer_spec(a, layer) for a in (w, b, wg, bg)],
        out_specs=[pl.BlockSpec((1, w.shape[1], nb), lambda t: (t, 0, 0)),
                   pl.BlockSpec((1, wg.shape[1], nb), lambda t: (t, 0, 0))],
        out_shape=[jax.ShapeDtypeStruct((ts, w.shape[1], nb), F32), jax.ShapeDtypeStruct((ts, wg.shape[1], nb), F32)],
        compiler_params=_params("arbitrary"),
        name="proj_t",
    )(x_tm, w, b, wg, bg)


def _layer_norm(x, g, b):
    mu = jnp.mean(x, axis=-1, keepdims=True)
    xc = x - mu
    var = jnp.mean(xc * xc, axis=-1, keepdims=True)
    return xc * lax.rsqrt(var + LN_EPS) * g + b


def _dense_kernel(x_ref, ya_ref, ys_ref, ym_ref, mixg_ref, wout_ref, ln1g_ref, ln1b_ref,
                  wgate_ref, wup_ref, wdown_ref, ln2g_ref, ln2b_ref, o_ref, *, alpha, ff_chunks):
    d_a = ya_ref.shape[-1]
    d_s = ys_ref.shape[-1]
    d_m = ym_ref.shape[-1]
    mixed = (_dot((ya_ref[...] * mixg_ref[:, :d_a]).astype(BF16), wout_ref[:d_a, :])
             + _dot((ys_ref[...] * mixg_ref[:, d_a:d_a + d_s]).astype(BF16), wout_ref[d_a:d_a + d_s, :])
             + _dot((ym_ref[...] * mixg_ref[:, d_a + d_s:]).astype(BF16), wout_ref[d_a + d_s:d_a + d_s + d_m, :]))
    x1 = _layer_norm(alpha * x_ref[...] + mixed, ln1g_ref[...], ln1b_ref[...])
    x1b = x1.astype(BF16)
    f = jnp.zeros_like(x1)
    c0 = 0
    for width in ff_chunks:
        gate = _dot(x1b, wgate_ref[:, c0:c0 + width])
        up = _dot(x1b, wup_ref[:, c0:c0 + width])
        hid = (gate * jax.nn.sigmoid(gate) * up).astype(BF16)
        f = f + _dot(hid, wdown_ref[c0:c0 + width, :])
        c0 += width
    o_ref[...] = _layer_norm(alpha * x1 + f, ln2g_ref[...], ln2b_ref[...])


def _dense(x, ya, ys, ym, lw, layer, *, alpha, d_ssm, seq_len=None):
    n, d = x.shape
    tm = _pick_tile(n, (256, 128, 64))
    d_ff = lw["w_gate"].shape[2]
    ff_chunks = []
    left = d_ff
    while left > 0:
        ff_chunks.append(min(1024, left))
        left -= ff_chunks[-1]
    row = lambda w: pl.BlockSpec((tm, w), lambda i: (i, 0))
    if seq_len is None:
        ys_spec = row(d_ssm)
    else:
        assert seq_len % tm == 0
        tps = seq_len // tm
        ys_spec = pl.BlockSpec((tm, d_ssm), lambda i: (i % tps, i // tps))
    consts = (lw["mix_g"], lw["w_out"], lw["ln1_g"], lw["ln1_b"], lw["w_gate"], lw["w_up"], lw["w_down"],
              lw["ln2_g"], lw["ln2_b"])
    return pl.pallas_call(
        functools.partial(_dense_kernel, alpha=alpha, ff_chunks=tuple(ff_chunks)),
        grid=(n // tm,),
        in_specs=[row(d), row(ya.shape[1]), ys_spec, row(ym.shape[1])]
        + [_layer_spec(c, layer) for c in consts],
        out_specs=row(d),
        out_shape=jax.ShapeDtypeStruct((n, d), F32),
        compiler_params=_params("arbitrary"),
        name="dense",
    )(x, ya, ys, ym, *consts)


def _block_diag(blocks):
    d, g, r, c = blocks.shape
    eye = jnp.eye(g, dtype=blocks.dtype)
    return (eye[None, :, None, :, None] * blocks[:, :, :, None, :]).reshape(d, g * r, g * c)


def kernel(x_prompt, x_sample, cache_k, cache_v, page_table, state_ssm_re, state_ssm_im, state_mlstm_c, state_mlstm_n, state_mlstm_m, w_in, b_in, mlstm_gate_b, mix_g, w_out, ln1_g, ln1_b, ssm_lambda_re, ssm_lambda_im, ssm_log_dt, ssm_b_re, ssm_b_im, ssm_c_re, ssm_c_im, ssm_d, ssm_w_glu, ssm_b_glu, w_gate, w_up, w_down, ln2_g, ln2_b):
    depth = w_in.shape[0]
    bp, tp, d_model = x_prompt.shape
    bs, ts, _ = x_sample.shape
    _, n_pool, page, h_a, hd_a = cache_k.shape
    d_attn = h_a * hd_a
    _, n_groups, n_p = ssm_lambda_re.shape
    cg = ssm_b_re.shape[-1]
    d_ssm = n_groups * cg
    n_state = n_groups * n_p
    _, _, h_m, hd_m, _ = state_mlstm_c.shape
    d_ml = h_m * hd_m
    d_main = 3 * d_attn + d_ssm + 4 * d_ml
    assert w_in.shape[2] == d_main + 2 * h_m and 2 * h_m <= SUBLANES
    q_scale = hd_a ** -0.5
    assert 2.0 ** round(math.log2(q_scale)) == q_scale
    alpha = (2 * depth) ** 0.25

    cache_k2 = jnp.transpose(cache_k, (0, 1, 3, 4, 2)).reshape(depth, n_pool, d_attn, page)
    cache_v2 = jnp.transpose(cache_v, (0, 1, 3, 4, 2)).reshape(depth, n_pool, d_attn, page)
    pt_flat = page_table.reshape(-1).astype(jnp.int32)
    c_state_t = jnp.transpose(state_mlstm_c, (0, 2, 3, 4, 1))

    xp = x_prompt.reshape(bp * tp, d_model)
    xs = x_sample.reshape(bs * ts, d_model)
    zero_state = jnp.zeros((bp, n_state), F32)
    zero_c = jnp.zeros((bp, h_m, hd_m, hd_m), F32)
    zero_n = jnp.zeros((bp, h_m, 1, hd_m), F32)
    zero_m = jnp.zeros((bp, h_m, 1, 1), F32)

    gate_pad = LANES - 2 * h_m
    rows = lambda a: a.reshape(depth, 1, -1).astype(F32)
    col = lambda a: a.reshape(depth, -1, 1).astype(F32)
    w_in_t = jnp.swapaxes(w_in, 1, 2)
    lw = {
        "w_main": w_in[:, :, :d_main].astype(BF16),
        "b_main": rows(b_in[:, :d_main]),
        "w_gate_in": jnp.pad(w_in[:, :, d_main:], ((0, 0), (0, 0), (0, gate_pad))).astype(BF16),
        "b_gate_in": rows(jnp.pad(b_in[:, d_main:] + mlstm_gate_b, ((0, 0), (0, gate_pad)))),
        "w_kvt": w_in_t[:, d_attn:3 * d_attn].astype(BF16),
        "b_kvt": col(b_in[:, d_attn:3 * d_attn]),
        "w_ml_t": w_in_t[:, d_main - 4 * d_ml:d_main].astype(BF16),
        "b_ml_t": col(b_in[:, d_main - 4 * d_ml:d_main]),
        "w_gate_t": w_in_t[:, d_main:].astype(BF16),
        "b_gate_t": col(b_in[:, d_main:] + mlstm_gate_b),
        "lam_re": rows(ssm_lambda_re),
        "lam_im": rows(ssm_lambda_im),
        "log_dt": rows(jnp.repeat(ssm_log_dt, n_p, axis=-1)),
        "braw_re": _block_diag(jnp.swapaxes(ssm_b_re, 2, 3)).astype(F32),
        "braw_im": _block_diag(jnp.swapaxes(ssm_b_im, 2, 3)).astype(F32),
        "c_re_bd": _block_diag(jnp.swapaxes(ssm_c_re, 2, 3)).astype(BF16),
        "c_im_bd": _block_diag(jnp.swapaxes(ssm_c_im, 2, 3)).astype(BF16),
        "d_skip": rows(ssm_d),
        "w_glu": ssm_w_glu.astype(BF16),
        "b_glu": rows(ssm_b_glu),
        "mix_g": rows(mix_g),
        "w_out": w_out.astype(BF16),
        "ln1_g": rows(ln1_g), "ln1_b": rows(ln1_b),
        "w_gate": w_gate.astype(BF16), "w_up": w_up.astype(BF16), "w_down": w_down.astype(BF16),
        "ln2_g": rows(ln2_g), "ln2_b": rows(ln2_b),
    }

    kv_all = (jnp.zeros((depth, bp * tp // page, d_attn, page), F32),) * 2
    outs_p = [[] for _ in range(5)]
    outs_s = [[] for _ in range(7)]
    for l in range(depth):
        proj = functools.partial(_proj, lw=lw, layer=l, d_attn=d_attn, d_ssm=d_ssm, d_ml=d_ml, q_scale=q_scale)

        q, *kv_all, kb, vb, us, qm, km, vm, om, gates = proj(xp, seq_len=tp, page=page, kv_all=kv_all)
        ya = _attn_prompt(q, kb, vb, bsz=bp, t=tp, hd=hd_a)
        ys, h_re, h_im = _s5(us.reshape(tp * bp, d_ssm), zero_state, zero_state, lw, l, t=tp)
        r3 = lambda a: a.reshape(bp, tp, -1)
        ym, c1, n1, m1 = _mlstm(r3(qm), r3(km), r3(vm), r3(om), r3(gates), zero_c, zero_n, zero_m, hd=hd_m)
        xp = _dense(xp, ya, ys.reshape(tp, bp * d_ssm), ym.reshape(bp * tp, d_ml), lw, l, alpha=alpha,
                    d_ssm=d_ssm, seq_len=tp)
        for lst, a in zip(outs_p, (h_re, h_im, c1, n1, m1)):
            lst.append(a)

        q, k, v, us = proj(xs)
        r3 = lambda a: a.reshape(bs, ts, -1)
        to_tm = lambda a: jnp.swapaxes(a.reshape(bs, ts, -1), 0, 1).reshape(ts * bs, -1)
        from_tm = lambda a: jnp.swapaxes(a.reshape(ts, bs, -1), 0, 1).reshape(bs * ts, -1)
        ya = _attn_sample(r3(q), r3(k), r3(v), cache_k2, cache_v2, pt_flat, l, n_heads=h_a, hd=hd_a)
        ys, h_re, h_im = _s5(to_tm(us), state_ssm_re[l].reshape(bs, n_state), state_ssm_im[l].reshape(bs, n_state),
                             lw, l, t=ts)
        x_t, g_t = _proj_t(to_tm(xs), lw, l, ts=ts)
        ym_t, c1, n1, m1 = _mlstm_seq(
            x_t.reshape(ts, 4, h_m, hd_m, bs),
            jnp.transpose(g_t.reshape(ts, 2, h_m, bs), (2, 0, 1, 3)),
            c_state_t, l, jnp.transpose(state_mlstm_n[l], (1, 2, 0)),
            state_mlstm_m[l].T.reshape(h_m, 1, bs), hd=hd_m)
        ym = jnp.transpose(ym_t.reshape(ts, d_ml, bs), (2, 0, 1)).reshape(bs * ts, d_ml)
        xs = _dense(xs, ya.reshape(bs * ts, d_attn), from_tm(ys), ym, lw, l, alpha=alpha, d_ssm=d_ssm)
        for lst, a in zip(outs_s, (k, v, h_re, h_im, c1, n1, m1)):
            lst.append(a)

    st = lambda lst: jnp.stack(lst)
    paged = lambda a: jnp.transpose(a.reshape(depth, bp, tp // page, h_a, hd_a, page), (0, 1, 2, 5, 3, 4))
    k_prompt, v_prompt = paged(kv_all[0]), paged(kv_all[1])
    k_sample = st(outs_s[0]).reshape(depth, bs, ts, h_a, hd_a)
    v_sample = st(outs_s[1]).reshape(depth, bs, ts, h_a, hd_a)
    ssm = lambda lst, b: st(lst).reshape(depth, b, n_groups, n_p)
    return (xp.reshape(bp, tp, d_model), xs.reshape(bs, ts, d_model),
            k_prompt, v_prompt, k_sample, v_sample,
            ssm(outs_p[0], bp), ssm(outs_p[1], bp), ssm(outs_s[2], bs), ssm(outs_s[3], bs),
            st(outs_p[2]), st(outs_p[3]).reshape(depth, bp, h_m, hd_m), st(outs_p[4]).reshape(depth, bp, h_m),
            jnp.transpose(st(outs_s[4]), (0, 4, 1, 2, 3)), jnp.transpose(st(outs_s[5]), (0, 3, 1, 2)),
            jnp.transpose(st(outs_s[6]).reshape(depth, h_m, bs), (0, 2, 1)))
```

```python
import functools
import math

import jax
import jax.numpy as jnp
from jax import lax
from jax.experimental import pallas as pl
from jax.experimental.pallas import tpu as pltpu

F32 = jnp.float32
BF16 = jnp.bfloat16

LN_EPS = 1e-5
RMS_EPS = 1e-6
V7X_VMEM_LIMIT_BYTES = 56 * 1024 * 1024
LANES = 128
SUBLANES = 8
ATTN_BLOCKS_PER_ITER = 3


def _dot(a, b):
    return jnp.dot(a, b, preferred_element_type=F32)


def _dot_nt(a, b):
    return lax.dot_general(a, b, (((1,), (1,)), ((), ())), preferred_element_type=F32)


def _dot_tn(a, b):
    return lax.dot_general(a, b, (((0,), (0,)), ((), ())), preferred_element_type=F32)


def _split2(x):
    hi = x.astype(BF16)
    lo = (x - hi.astype(F32)).astype(BF16)
    return hi, lo


def _tail_sum_matrix(k):
    return ((_iota((2 * k, k), 0) & (k - 1)) >= _iota((2 * k, k), 1)).astype(BF16)


def _tail_sum(x, u2):
    return _dot(jnp.concatenate(_split2(x), axis=-1), u2)


def _softplus(z):
    return jnp.maximum(z, 0.0) + jnp.log(1.0 + jnp.exp(-jnp.abs(z)))


def _split3(x):
    hi = x.astype(BF16)
    r = x - hi.astype(F32)
    mid = r.astype(BF16)
    lo = (r - mid.astype(F32)).astype(BF16)
    return hi, mid, lo


def _iota(shape, dim):
    return lax.broadcasted_iota(jnp.int32, shape, dim)


def _log2(n):
    k = int(math.log2(n))
    assert 1 << k == n, n
    return k


def _pick_tile(n, candidates):
    for c in candidates:
        if n % c == 0:
            return c
    raise ValueError(f"no tile for {n} in {candidates}")


def _params(*sem):
    return pltpu.CompilerParams(dimension_semantics=sem, vmem_limit_bytes=V7X_VMEM_LIMIT_BYTES)


def _const_spec(shape):
    nd = len(shape)
    return pl.BlockSpec(shape, lambda *_: (0,) * nd)


def _layer_spec(stacked, layer):
    nd = stacked.ndim - 1
    return pl.BlockSpec((None,) + stacked.shape[1:], lambda *_: (layer,) + (0,) * nd)


def _proj_kernel(x_ref, w_ref, b_ref, wg_ref, bg_ref, *rest, d_attn, d_ssm, d_ml, q_scale, paged):
    xb = x_ref[...].astype(BF16)

    def cols(c0, width):
        return _dot(xb, w_ref[:, c0:c0 + width]) + b_ref[:, c0:c0 + width]

    if paged:
        wkvt_ref, bkvt_ref, _, _, q_ref, kp_ref, vp_ref, kpb_ref, vpb_ref = rest[:9]
        kvt = _dot_nt(wkvt_ref[...], xb) + bkvt_ref[...]
        page = kp_ref.shape[-1]
        for j in range(kp_ref.shape[0]):
            kt = kvt[:d_attn, j * page:(j + 1) * page]
            vt = kvt[d_attn:, j * page:(j + 1) * page]
            kp_ref[j] = kt
            vp_ref[j] = vt
            kpb_ref[j] = kt.astype(BF16)
            vpb_ref[j] = vt.astype(BF16)
        rest = rest[9:]
    else:
        q_ref, k_ref, v_ref = rest[:3]
        k_ref[...] = cols(d_attn, d_attn)
        v_ref[...] = cols(2 * d_attn, d_attn)
        rest = rest[3:]
    q_ref[...] = cols(0, d_attn) * q_scale
    c = 3 * d_attn
    rest[0][...] = cols(c, d_ssm)
    if len(rest) > 1:
        qm_ref, km_ref, vm_ref, om_ref, g_ref = rest[1:]
        c += d_ssm
        for ref in (qm_ref, km_ref, vm_ref, om_ref):
            ref[...] = cols(c, d_ml)
            c += d_ml
        g_ref[...] = _dot(xb, wg_ref[...]) + bg_ref[...]


def _proj(x, lw, layer, *, d_attn, d_ssm, d_ml, q_scale, seq_len=None, page=None, kv_all=None):
    n, d = x.shape
    tm = _pick_tile(n, (512, 256, 128, 64))
    row = lambda w, dt=F32: (pl.BlockSpec((tm, w), lambda i: (i, 0)), jax.ShapeDtypeStruct((n, w), dt))
    ins = [x, lw["w_main"], lw["b_main"], lw["w_gate_in"], lw["b_gate_in"]]
    in_specs = [pl.BlockSpec((tm, d), lambda i: (i, 0))] + [_layer_spec(a, layer) for a in ins[1:]]
    aliases = {}
    if page is None:
        outs = [row(d_attn), row(d_attn), row(d_attn), row(d_ssm)]
    else:
        assert tm % page == 0 and seq_len % tm == 0
        ins += [lw["w_kvt"], lw["b_kvt"], *kv_all]
        in_specs += [_layer_spec(lw["w_kvt"], layer), _layer_spec(lw["b_kvt"], layer),
                     pl.BlockSpec(memory_space=pl.ANY), pl.BlockSpec(memory_space=pl.ANY)]
        aliases = {len(ins) - 2: 1, len(ins) - 1: 2}
        slab_all = (pl.BlockSpec((None, tm // page, d_attn, page), lambda i: (layer, i, 0, 0)),
                    jax.ShapeDtypeStruct(kv_all[0].shape, F32))
        slab = lambda dt: (pl.BlockSpec((tm // page, d_attn, page), lambda i: (i, 0, 0)),
                           jax.ShapeDtypeStruct((n // page, d_attn, page), dt))
        tps = seq_len // tm
        us_tm = (pl.BlockSpec((tm, d_ssm), lambda i: (i % tps, i // tps)),
                 jax.ShapeDtypeStruct((seq_len, n // seq_len * d_ssm), F32))
        outs = [row(d_attn), slab_all, slab_all, slab(BF16), slab(BF16), us_tm,
                row(d_ml), row(d_ml), row(d_ml), row(d_ml), row(LANES)]
    return pl.pallas_call(
        functools.partial(_proj_kernel, d_attn=d_attn, d_ssm=d_ssm, d_ml=d_ml, q_scale=q_scale,
                          paged=page is not None),
        grid=(n // tm,),
        in_specs=in_specs,
        out_specs=[o[0] for o in outs],
        out_shape=[o[1] for o in outs],
        input_output_aliases=aliases,
        compiler_params=_params("arbitrary"),
        name="proj",
    )(*ins)


def _sb_weights(zs, carries, u2, masks):
    sps = []
    for z, m in zip(zs, masks):
        sp = _softplus(z)
        sps.append(sp if m is None else jnp.where(m, sp, 0.0))
    tails = [_tail_sum(sp, u2) for sp in sps]
    ws, new_carries = [], []
    for z, sp, tail, carry, m in zip(zs, sps, tails, carries, masks):
        w = jnp.exp(z - tail - carry)
        ws.append(w if m is None else jnp.where(m, w, 0.0))
        new_carries.append(carry + jnp.sum(sp, axis=-1, keepdims=True))
    return ws, new_carries


def _attn_prompt_kernel(q_ref, k_ref, v_ref, o_ref, *, tq, tk, n_heads, hd):
    qi = pl.program_id(1)
    page = k_ref.shape[-1]
    ppb = tk // page
    u2 = _tail_sum_matrix(tk)
    heads = [slice(h * hd, (h + 1) * hd) for h in range(n_heads)]
    q = q_ref[...]
    qs = [q[:, sl].astype(BF16) for sl in heads]

    def blocks(jbs, carries, accs, mask):
        def kv(ref, j, sl):
            return jnp.concatenate([ref[j * ppb + i, sl, :] for i in range(ppb)], axis=-1)
        zs = [[_dot(qs[h], kv(k_ref, j, heads[h])) for h in range(n_heads)] for j in jbs]
        sps = [[_softplus(z) if mask is None else jnp.where(mask, _softplus(z), 0.0) for z in zz] for zz in zs]
        tails = [[_tail_sum(sp, u2) for sp in ss] for ss in sps]
        for zz, ss, tl, j in zip(zs, sps, tails, jbs):
            ws = [jnp.exp(z - t - c) for z, t, c in zip(zz, tl, carries)]
            if mask is not None:
                ws = [jnp.where(mask, w, 0.0) for w in ws]
            carries = [c + jnp.sum(sp, axis=-1, keepdims=True) for c, sp in zip(carries, ss)]
            accs = [acc + _dot_nt(w.astype(BF16), kv(v_ref, j, heads[h]))
                    for h, (acc, w) in enumerate(zip(accs, ws))]
        return carries, accs

    jd = (qi * tq) // tk
    offset = qi * tq - jd * tk
    diag_mask = _iota((tq, tk), 1) < _iota((tq, tk), 0) + offset
    carries = [jnp.zeros((tq, 1), F32)] * n_heads
    accs = [jnp.zeros((tq, hd), F32)] * n_heads
    carries, accs = blocks([jd], carries, accs, diag_mask)

    def loop(n_iter, first, per_iter, ca):
        def body(i, ca):
            top = first - i * per_iter
            carries, accs = blocks([top - r for r in range(per_iter)], list(ca[0]), list(ca[1]), None)
            return tuple(carries), tuple(accs)
        return lax.fori_loop(0, n_iter, body, ca)

    ca = loop(jd // ATTN_BLOCKS_PER_ITER, jd - 1, ATTN_BLOCKS_PER_ITER, (tuple(carries), tuple(accs)))
    carries, accs = loop(jd % ATTN_BLOCKS_PER_ITER, jd % ATTN_BLOCKS_PER_ITER - 1, 1, ca)
    outs = [acc * lax.rsqrt(jnp.mean(acc * acc, axis=-1, keepdims=True) + RMS_EPS) for acc in accs]
    o_ref[...] = jnp.concatenate(outs, axis=-1)


def _attn_prompt(q, kpages, vpages, *, bsz, t, hd):
    n, d_attn = q.shape
    n_pages, _, page = kpages.shape
    assert n == bsz * t and n_pages * page == n
    tq = _pick_tile(t, (256, 128, 64))
    tk = _pick_tile(t, (256, 128))
    assert tk % tq == 0 and tk % page == 0
    nq = t // tq
    ppb = t // page
    return pl.pallas_call(
        functools.partial(_attn_prompt_kernel, tq=tq, tk=tk, n_heads=d_attn // hd, hd=hd),
        grid=(bsz, nq),
        in_specs=[pl.BlockSpec((tq, d_attn), lambda b, i: (b * nq + i, 0)),
                  pl.BlockSpec((ppb, d_attn, page), lambda b, i: (b, 0, 0)),
                  pl.BlockSpec((ppb, d_attn, page), lambda b, i: (b, 0, 0))],
        out_specs=pl.BlockSpec((tq, d_attn), lambda b, i: (b * nq + i, 0)),
        out_shape=jax.ShapeDtypeStruct((n, d_attn), F32),
        compiler_params=_params("arbitrary", "arbitrary"),
        name="attn_prompt",
    )(q, kpages, vpages)


def _attn_sample_kernel(pt_ref, q_ref, kn_ref, vn_ref, *rest, n_pages, page, n_heads, hd, ts, bb):
    del pt_ref
    n_in = bb * n_pages
    kp_refs = rest[:n_in]
    vp_refs = rest[n_in:2 * n_in]
    o_ref = rest[2 * n_in]
    rows = n_heads * ts
    d_attn = n_heads * hd
    ts_shift, hd_shift = _log2(ts), _log2(hd)

    head_mask = (_iota((rows, d_attn), 0) >> ts_shift) == (_iota((rows, d_attn), 1) >> hd_shift)
    qbd = [jnp.where(head_mask, jnp.concatenate([q_ref[r]] * n_heads, axis=0), 0.0).astype(BF16)
           for r in range(bb)]
    u2 = _tail_sum_matrix(page)

    pad = jnp.zeros((page - ts, d_attn), F32)
    k_new = [jnp.concatenate([kn_ref[r], pad], axis=0).astype(BF16) for r in range(bb)]
    v_new = [jnp.concatenate([vn_ref[r], pad], axis=0).astype(BF16) for r in range(bb)]
    new_mask = _iota((rows, page), 1) < (_iota((rows, page), 0) & (ts - 1))
    zero = jnp.zeros((rows, 1), F32)
    ws, carries = _sb_weights([_dot_nt(qbd[r], k_new[r]) for r in range(bb)], [zero] * bb, u2, [new_mask] * bb)
    accs = [_dot(w.astype(BF16), v) for w, v in zip(ws, v_new)]
    order = list(reversed(range(n_pages)))
    zs = [[_dot(qbd[r], kp_refs[r * n_pages + p][0, 0].astype(BF16)) for p in order] for r in range(bb)]
    sps = [[_softplus(z) for z in zz] for zz in zs]
    tails = [[_tail_sum(sp, u2) for sp in ss] for ss in sps]
    col_head = _iota((ts, d_attn), 1) >> hd_shift
    for r in range(bb):
        carry, acc = carries[r], accs[r]
        for z, sp, tail, p in zip(zs[r], sps[r], tails[r], order):
            w = jnp.exp(z - tail - carry)
            carry = carry + jnp.sum(sp, axis=-1, keepdims=True)
            acc = acc + _dot_nt(w.astype(BF16), vp_refs[r * n_pages + p][0, 0].astype(BF16))

        y = jnp.sum(jnp.where(head_mask, acc, 0.0).reshape(n_heads, ts, d_attn), axis=0)
        sq = y * y
        scale = jnp.zeros((ts, d_attn), F32)
        for h in range(n_heads):
            sel = col_head == h
            ms = jnp.sum(jnp.where(sel, sq, 0.0), axis=-1, keepdims=True) * (1.0 / hd)
            scale = jnp.where(sel, lax.rsqrt(ms + RMS_EPS), scale)
        o_ref[r] = y * scale


def _attn_sample(q, k_new, v_new, cache_k, cache_v, page_table_flat, layer, *, n_heads, hd):
    bsz, ts, d_attn = q.shape
    n_pages = page_table_flat.shape[0] // bsz
    page = cache_k.shape[3]
    assert ts <= page and ts % SUBLANES == 0
    bb = _pick_tile(bsz, (2, 1))

    def page_spec(r, p):
        return pl.BlockSpec((1, 1, d_attn, page), lambda b, pt: (layer, pt[(b * bb + r) * n_pages + p], 0, 0))

    tok_spec = pl.BlockSpec((bb, ts, d_attn), lambda b, pt: (b, 0, 0))
    pages = [page_spec(r, p) for r in range(bb) for p in range(n_pages)]
    grid_spec = pltpu.PrefetchScalarGridSpec(
        num_scalar_prefetch=1,
        grid=(bsz // bb,),
        in_specs=[tok_spec, tok_spec, tok_spec] + pages + pages,
        out_specs=tok_spec,
    )
    return pl.pallas_call(
        functools.partial(_attn_sample_kernel, n_pages=n_pages, page=page, n_heads=n_heads, hd=hd, ts=ts, bb=bb),
        grid_spec=grid_spec,
        out_shape=jax.ShapeDtypeStruct((bsz, ts, d_attn), F32),
        compiler_params=_params("arbitrary"),
        name="attn_sample",
    )(page_table_flat, q, k_new, v_new, *([cache_k] * (bb * n_pages)), *([cache_v] * (bb * n_pages)))


def _gelu_tanh(x):
    return 0.5 * x * (1.0 + jnp.tanh(math.sqrt(2.0 / math.pi) * (x + 0.044715 * (x * x * x))))


def _s5_kernel(u_ref, h0re_ref, h0im_ref, lamre_ref, lamim_ref, logdt_ref, braw_re_ref, braw_im_ref,
               cre_ref, cim_ref, d_ref, wglu_ref, bglu_ref,
               y_ref, hre_out_ref, him_out_ref,
               bu_re, bu_im, hs_re, hs_im, a_re_s, a_im_s, bbd_re, bbd_im,
               *, nb, tt, n_steps):
    step = pl.program_id(0)

    @pl.when(step == 0)
    def _():
        lam_re = lamre_ref[...]
        lam_im = lamim_ref[...]
        dt = jnp.exp(logdt_ref[...])
        mag = jnp.exp(lam_re * dt)
        a_re = mag * jnp.cos(lam_im * dt)
        a_im = mag * jnp.sin(lam_im * dt)
        den = lam_re * lam_re + lam_im * lam_im
        z_re = ((a_re - 1.0) * lam_re + a_im * lam_im) / den
        z_im = (a_im * lam_re - (a_re - 1.0) * lam_im) / den
        a_re_s[...] = a_re
        a_im_s[...] = a_im
        bbd_re[...] = (z_re * braw_re_ref[...] - z_im * braw_im_ref[...]).astype(BF16)
        bbd_im[...] = (z_re * braw_im_ref[...] + z_im * braw_re_ref[...]).astype(BF16)
        hs_re[...] = h0re_ref[...]
        hs_im[...] = h0im_ref[...]

    d_ssm = u_ref.shape[-1]
    u = u_ref[...]
    ub = u.astype(BF16)
    n_lb = bu_re.shape[0]
    lanes = [slice(j * LANES, (j + 1) * LANES) for j in range(n_lb)]
    for buf, bbd in ((bu_re, bbd_re), (bu_im, bbd_im)):
        full = _dot(ub, bbd[...])
        for j in range(n_lb):
            buf[j] = full[:, lanes[j]]

    a_re = [a_re_s[:, lanes[j]] for j in range(n_lb)]
    a_im = [a_im_s[:, lanes[j]] for j in range(n_lb)]
    for b0 in range(0, nb, SUBLANES):
        gs = min(SUBLANES, nb - b0)

        def scan_step(t, carry):
            start = t * nb + b0
            rows = pl.ds(pl.multiple_of(start, SUBLANES) if gs == SUBLANES and nb % SUBLANES == 0 else start, gs)
            out = []
            for j in range(n_lb):
                h_re, h_im = carry[2 * j], carry[2 * j + 1]
                n_re = a_re[j] * h_re - a_im[j] * h_im + bu_re[j, rows, :]
                n_im = a_re[j] * h_im + a_im[j] * h_re + bu_im[j, rows, :]
                bu_re[j, rows, :] = n_re
                bu_im[j, rows, :] = n_im
                out += [n_re, n_im]
            return tuple(out)

        init = []
        for j in range(n_lb):
            init += [hs_re[b0:b0 + gs, lanes[j]], hs_im[b0:b0 + gs, lanes[j]]]
        fin = lax.fori_loop(0, tt, scan_step, tuple(init), unroll=min(tt, 8))
        for j in range(n_lb):
            hs_re[b0:b0 + gs, lanes[j]] = fin[2 * j]
            hs_im[b0:b0 + gs, lanes[j]] = fin[2 * j + 1]

    states = lambda buf: jnp.concatenate([buf[j] for j in range(n_lb)], axis=-1).astype(BF16)
    y = _dot(states(bu_re), cre_ref[...]) - _dot(states(bu_im), cim_ref[...]) + d_ref[...] * u
    g = _gelu_tanh(y)
    ys = g * jax.nn.sigmoid(_dot(g.astype(BF16), wglu_ref[...]) + bglu_ref[...])
    ys = ys * lax.rsqrt(jnp.mean(ys * ys, axis=-1, keepdims=True) + RMS_EPS)
    y_ref[...] = ys

    @pl.when(step == n_steps - 1)
    def _():
        hre_out_ref[...] = hs_re[...]
        him_out_ref[...] = hs_im[...]


def _s5(u_tm, h0_re, h0_im, lw, layer, *, t):
    n, d_ssm = u_tm.shape
    nb = n // t
    n_state = h0_re.shape[-1]
    tt = _pick_tile(t, (512, 256, 128, 64, 8))
    n_steps = t // tt
    consts = (lw["lam_re"], lw["lam_im"], lw["log_dt"], lw["braw_re"], lw["braw_im"],
              lw["c_re_bd"], lw["c_im_bd"], lw["d_skip"], lw["w_glu"], lw["b_glu"])
    u_spec = pl.BlockSpec((tt * nb, d_ssm), lambda i: (i, 0))
    return pl.pallas_call(
        functools.partial(_s5_kernel, nb=nb, tt=tt, n_steps=n_steps),
        grid=(n_steps,),
        in_specs=[u_spec, _const_spec(h0_re.shape), _const_spec(h0_im.shape)]
        + [_layer_spec(c, layer) for c in consts],
        out_specs=[u_spec, _const_spec(h0_re.shape), _const_spec(h0_im.shape)],
        out_shape=[jax.ShapeDtypeStruct(u_tm.shape, F32),
                   jax.ShapeDtypeStruct(h0_re.shape, F32), jax.ShapeDtypeStruct(h0_im.shape, F32)],
        scratch_shapes=[pltpu.VMEM((n_state // LANES, nb * tt, LANES), F32),
                        pltpu.VMEM((n_state // LANES, nb * tt, LANES), F32),
                        pltpu.VMEM((nb, n_state), F32), pltpu.VMEM((nb, n_state), F32),
                        pltpu.VMEM((1, n_state), F32), pltpu.VMEM((1, n_state), F32),
                        pltpu.VMEM((d_ssm, n_state), BF16), pltpu.VMEM((d_ssm, n_state), BF16)],
        compiler_params=_params("arbitrary"),
        name="s5",
    )(u_tm, h0_re, h0_im, *consts)


def _mlstm_kernel(q_ref, k_ref, v_ref, o_ref, g_ref, c0_ref, n0_ref, m0_ref,
                  y_ref, c1_ref, n1_ref, m1_ref, c_s, n_s, m_s,
                  *, chunk, n_heads, hd, n_chunks):
    ci = pl.program_id(1)

    @pl.when(ci == 0)
    def _():
        c_s[...] = c0_ref[0]
        n_s[...] = n0_ref[0]
        m_s[...] = m0_ref[0]

    L = chunk
    lane = _iota((L, LANES), 1)
    tril = _iota((L, L), 0) >= _iota((L, L), 1)
    tril_b = tril.astype(BF16)
    triu_b = (_iota((L, L), 0) <= _iota((L, L), 1)).astype(BF16)
    pick = (_iota((SUBLANES, LANES), 0) == _iota((SUBLANES, LANES), 1)).astype(BF16)
    k_scale = hd ** -0.5

    gates = g_ref[0]
    is_f = (lane >= n_heads) & (lane < 2 * n_heads)
    gl = jnp.where(is_f, jax.nn.log_sigmoid(gates), gates)
    parts = _split3(gl)
    cum_col = sum(_dot(tril_b, p) for p in parts)
    g_row = sum(_dot_nt(pick, p) for p in parts)
    cum_row = sum(_dot(p, triu_b) for p in _split3(g_row))

    outs = []
    for h in range(n_heads):
        sl = slice(h * hd, (h + 1) * hd)
        q = q_ref[0, :, sl]
        k = k_ref[0, :, sl] * k_scale
        v = v_ref[0, :, sl]
        qb, kb = q.astype(BF16), k.astype(BF16)
        c_prev = c_s[h]
        n_prev = n_s[h]
        m_prev = m_s[h]

        i_col = gl[:, h:h + 1]
        b_col = cum_col[:, n_heads + h:n_heads + h + 1]
        i_row = g_row[h:h + 1, :]
        b_row = cum_row[n_heads + h:n_heads + h + 1, :]

        log_d = jnp.where(tril, b_col - b_row + i_row, -jnp.inf)
        log_inter = b_col + m_prev
        m_t = jnp.maximum(log_inter, jnp.max(log_d, axis=-1, keepdims=True))
        s = _dot_nt(qb, kb) * jnp.exp(log_d - m_t)
        w_inter = jnp.exp(log_inter - m_t)
        num = _dot(s.astype(BF16), v.astype(BF16)) + w_inter * _dot_nt(qb, c_prev.astype(BF16))
        den = jnp.sum(s, axis=-1, keepdims=True) + w_inter * jnp.sum(q * n_prev, axis=-1, keepdims=True)
        h_t = num / jnp.maximum(jnp.abs(den), jnp.exp(-m_t))

        m_new = m_t[L - 1:L, :]
        b_last = b_col[L - 1:L, :]
        w_s = jnp.exp(b_last - b_col + i_col - m_new)
        decay = jnp.exp(b_last + m_prev - m_new)
        c_s[h] = decay * c_prev + _dot_tn((w_s * v).astype(BF16), kb)
        n_s[h] = decay * n_prev + jnp.sum(w_s * k, axis=0, keepdims=True)
        m_s[h] = m_new

        y = h_t * jax.nn.sigmoid(o_ref[0, :, sl])
        outs.append(y * lax.rsqrt(jnp.mean(y * y, axis=-1, keepdims=True) + RMS_EPS))
    y_ref[0] = jnp.concatenate(outs, axis=-1)

    @pl.when(ci == n_chunks - 1)
    def _():
        c1_ref[0] = c_s[...]
        n1_ref[0] = n_s[...]
        m1_ref[0] = m_s[...]


def _mlstm(q, k, v, o, gates, c0, n0, m0, *, hd):
    bsz, t, d_ml = q.shape
    n_heads = d_ml // hd
    chunk = _pick_tile(t, (256, 128, 64, 8))
    n_chunks = t // chunk
    tok = lambda w: pl.BlockSpec((1, chunk, w), lambda b, c: (b, c, 0))
    st = lambda shape: pl.BlockSpec((1,) + shape[1:], lambda b, c: (b,) + (0,) * (len(shape) - 1))
    return pl.pallas_call(
        functools.partial(_mlstm_kernel, chunk=chunk, n_heads=n_heads, hd=hd, n_chunks=n_chunks),
        grid=(bsz, n_chunks),
        in_specs=[tok(d_ml), tok(d_ml), tok(d_ml), tok(d_ml), tok(LANES),
                  st(c0.shape), st(n0.shape), st(m0.shape)],
        out_specs=[tok(d_ml), st(c0.shape), st(n0.shape), st(m0.shape)],
        out_shape=[jax.ShapeDtypeStruct(q.shape, F32), jax.ShapeDtypeStruct(c0.shape, F32),
                   jax.ShapeDtypeStruct(n0.shape, F32), jax.ShapeDtypeStruct(m0.shape, F32)],
        scratch_shapes=[pltpu.VMEM(c0.shape[1:], F32), pltpu.VMEM(n0.shape[1:], F32), pltpu.VMEM(m0.shape[1:], F32)],
        compiler_params=_params("arbitrary", "arbitrary"),
        name="mlstm",
    )(q, k, v, o, gates, c0, n0, m0)


def _mlstm_seq_kernel(x_ref, g_ref, c0_ref, n0_ref, m0_ref, y_ref, c1_ref, n1_ref, m1_ref, hnum_s, *, ts, hd):
    k_scale = hd ** -0.5
    q = [x_ref[t, 0, 0] for t in range(ts)]
    k = [x_ref[t, 1, 0] * k_scale for t in range(ts)]
    m = m0_ref[0]
    n = n0_ref[0]
    fp, ip, den = [], [], []
    for t in range(ts):
        i_t = g_ref[0, t, 0:1, :]
        lf = jax.nn.log_sigmoid(g_ref[0, t, 1:2, :])
        m_new = jnp.maximum(lf + m, i_t)
        fp.append(jnp.exp(lf + m - m_new))
        ip.append(jnp.exp(i_t - m_new))
        m = m_new
        n = fp[t] * n + ip[t] * k[t]
        den.append(jnp.maximum(jnp.abs(jnp.sum(n * q[t], axis=0, keepdims=True)), jnp.exp(-m)))
    n1_ref[0] = n
    m1_ref[0] = m

    def rows(vb, carry):
        v0 = pl.multiple_of(vb * SUBLANES, SUBLANES)
        hn = [[None] * SUBLANES for _ in range(ts)]
        for r in range(SUBLANES):
            c = c0_ref[0, v0 + r]
            for t in range(ts):
                c = fp[t] * c + (ip[t] * x_ref[t, 2, 0, pl.ds(v0 + r, 1), :]) * k[t]
                hn[t][r] = jnp.sum(c * q[t], axis=0, keepdims=True)
            c1_ref[0, v0 + r] = c
        for t in range(ts):
            hnum_s[t, pl.ds(v0, SUBLANES), :] = jnp.concatenate(hn[t], axis=0)
        return carry

    lax.fori_loop(0, hd // SUBLANES, rows, 0)
    for t in range(ts):
        y = hnum_s[t] / den[t] * jax.nn.sigmoid(x_ref[t, 3, 0])
        y_ref[t, 0] = y * lax.rsqrt(jnp.mean(y * y, axis=0, keepdims=True) + RMS_EPS)


def _mlstm_seq(x, gates, c_all, layer, n0, m0, *, hd):
    ts, _, n_heads, _, nb = x.shape
    c0 = jax.ShapeDtypeStruct(c_all.shape[1:], F32)
    head = lambda shape: pl.BlockSpec((1,) + shape[1:], lambda h: (h,) + (0,) * (len(shape) - 1))
    y_spec = pl.BlockSpec((ts, 1, hd, nb), lambda h: (0, h, 0, 0))
    return pl.pallas_call(
        functools.partial(_mlstm_seq_kernel, ts=ts, hd=hd),
        grid=(n_heads,),
        in_specs=[pl.BlockSpec((ts, 4, 1, hd, nb), lambda h: (0, 0, h, 0, 0)), head(gates.shape),
                  pl.BlockSpec((None, 1, hd, hd, nb), lambda h: (layer, h, 0, 0, 0)),
                  head(n0.shape), head(m0.shape)],
        out_specs=[y_spec, head(c0.shape), head(n0.shape), head(m0.shape)],
        out_shape=[jax.ShapeDtypeStruct((ts, n_heads, hd, nb), F32), jax.ShapeDtypeStruct(c0.shape, F32),
                   jax.ShapeDtypeStruct(n0.shape, F32), jax.ShapeDtypeStruct(m0.shape, F32)],
        scratch_shapes=[pltpu.VMEM((ts, hd, nb), F32)],
        compiler_params=_params("arbitrary"),
        name="mlstm_seq",
    )(x, gates, c_all, n0, m0)


def _proj_t_kernel(x_ref, w_ref, b_ref, wg_ref, bg_ref, o_ref, g_ref):
    xb = x_ref[...].astype(BF16)
    o_ref[0] = _dot_nt(w_ref[...], xb) + b_ref[...]
    g_ref[0] = _dot_nt(wg_ref[...], xb) + bg_ref[...]


def _proj_t(x_tm, lw, layer, *, ts):
    n, d = x_tm.shape
    nb = n // ts
    w, b, wg, bg = lw["w_ml_t"], lw["b_ml_t"], lw["w_gate_t"], lw["b_gate_t"]
    return pl.pallas_call(
        _proj_t_kernel,
        grid=(ts,),
        in_specs=[pl.BlockSpec((nb, d), lambda t: (t, 0))] + [_layer_spec(a, layer) for a in (w, b, wg, bg)],
        out_specs=[pl.BlockSpec((1, w.shape[1], nb), lambda t: (t, 0, 0)),
                   pl.BlockSpec((1, wg.shape[1], nb), lambda t: (t, 0, 0))],
        out_shape=[jax.ShapeDtypeStruct((ts, w.shape[1], nb), F32), jax.ShapeDtypeStruct((ts, wg.shape[1], nb), F32)],
        compiler_params=_params("arbitrary"),
        name="proj_t",
    )(x_tm, w, b, wg, bg)


def _layer_norm(x, g, b):
    mu = jnp.mean(x, axis=-1, keepdims=True)
    xc = x - mu
    var = jnp.mean(xc * xc, axis=-1, keepdims=True)
    return xc * lax.rsqrt(var + LN_EPS) * g + b


def _dense_kernel(x_ref, ya_ref, ys_ref, ym_ref, mixg_ref, wout_ref, ln1g_ref, ln1b_ref,
                  wgate_ref, wup_ref, wdown_ref, ln2g_ref, ln2b_ref, o_ref, *, alpha, ff_chunks):
    d_a = ya_ref.shape[-1]
    d_s = ys_ref.shape[-1]
    d_m = ym_ref.shape[-1]
    mixed = (_dot((ya_ref[...] * mixg_ref[:, :d_a]).astype(BF16), wout_ref[:d_a, :])
             + _dot((ys_ref[...] * mixg_ref[:, d_a:d_a + d_s]).astype(BF16), wout_ref[d_a:d_a + d_s, :])
             + _dot((ym_ref[...] * mixg_ref[:, d_a + d_s:]).astype(BF16), wout_ref[d_a + d_s:d_a + d_s + d_m, :]))
    x1 = _layer_norm(alpha * x_ref[...] + mixed, ln1g_ref[...], ln1b_ref[...])
    x1b = x1.astype(BF16)
    f = jnp.zeros_like(x1)
    c0 = 0
    for width in ff_chunks:
        gate = _dot(x1b, wgate_ref[:, c0:c0 + width])
        up = _dot(x1b, wup_ref[:, c0:c0 + width])
        hid = (gate * jax.nn.sigmoid(gate) * up).astype(BF16)
        f = f + _dot(hid, wdown_ref[c0:c0 + width, :])
        c0 += width
    o_ref[...] = _layer_norm(alpha * x1 + f, ln2g_ref[...], ln2b_ref[...])


def _dense(x, ya, ys, ym, lw, layer, *, alpha, d_ssm, seq_len=None):
    n, d = x.shape
    tm = _pick_tile(n, (256, 128, 64))
    d_ff = lw["w_gate"].shape[2]
    ff_chunks = []
    left = d_ff
    while left > 0:
        ff_chunks.append(min(1024, left))
        left -= ff_chunks[-1]
    row = lambda w: pl.BlockSpec((tm, w), lambda i: (i, 0))
    if seq_len is None:
        ys_spec = row(d_ssm)
    else:
        assert seq_len % tm == 0
        tps = seq_len // tm
        ys_spec = pl.BlockSpec((tm, d_ssm), lambda i: (i % tps, i // tps))
    consts = (lw["mix_g"], lw["w_out"], lw["ln1_g"], lw["ln1_b"], lw["w_gate"], lw["w_up"], lw["w_down"],
              lw["ln2_g"], lw["ln2_b"])
    return pl.pallas_call(
        functools.partial(_dense_kernel, alpha=alpha, ff_chunks=tuple(ff_chunks)),
        grid=(n // tm,),
        in_specs=[row(d), row(ya.shape[1]), ys_spec, row(ym.shape[1])]
        + [_layer_spec(c, layer) for c in consts],
        out_specs=row(d),
        out_shape=jax.ShapeDtypeStruct((n, d), F32),
        compiler_params=_params("arbitrary"),
        name="dense",
    )(x, ya, ys, ym, *consts)


def _block_diag(blocks):
    d, g, r, c = blocks.shape
    eye = jnp.eye(g, dtype=blocks.dtype)
    return (eye[None, :, None, :, None] * blocks[:, :, :, None, :]).reshape(d, g * r, g * c)


def kernel(x_prompt, x_sample, cache_k, cache_v, page_table, state_ssm_re, state_ssm_im, state_mlstm_c, state_mlstm_n, state_mlstm_m, w_in, b_in, mlstm_gate_b, mix_g, w_out, ln1_g, ln1_b, ssm_lambda_re, ssm_lambda_im, ssm_log_dt, ssm_b_re, ssm_b_im, ssm_c_re, ssm_c_im, ssm_d, ssm_w_glu, ssm_b_glu, w_gate, w_up, w_down, ln2_g, ln2_b):
    depth = w_in.shape[0]
    bp, tp, d_model = x_prompt.shape
    bs, ts, _ = x_sample.shape
    _, n_pool, page, h_a, hd_a = cache_k.shape
    d_attn = h_a * hd_a
    _, n_groups, n_p = ssm_lambda_re.shape
    cg = ssm_b_re.shape[-1]
    d_ssm = n_groups * cg
    n_state = n_groups * n_p
    _, _, h_m, hd_m, _ = state_mlstm_c.shape
    d_ml = h_m * hd_m
    d_main = 3 * d_attn + d_ssm + 4 * d_ml
    assert w_in.shape[2] == d_main + 2 * h_m and 2 * h_m <= SUBLANES
    q_scale = hd_a ** -0.5
    assert 2.0 ** round(math.log2(q_scale)) == q_scale
    alpha = (2 * depth) ** 0.25

    cache_k2 = jnp.transpose(cache_k, (0, 1, 3, 4, 2)).reshape(depth, n_pool, d_attn, page)
    cache_v2 = jnp.transpose(cache_v, (0, 1, 3, 4, 2)).reshape(depth, n_pool, d_attn, page)
    pt_flat = page_table.reshape(-1).astype(jnp.int32)
    c_state_t = jnp.transpose(state_mlstm_c, (0, 2, 3, 4, 1))

    xp = x_prompt.reshape(bp * tp, d_model)
    xs = x_sample.reshape(bs * ts, d_model)
    zero_state = jnp.zeros((bp, n_state), F32)
    zero_c = jnp.zeros((bp, h_m, hd_m, hd_m), F32)
    zero_n = jnp.zeros((bp, h_m, 1, hd_m), F32)
    zero_m = jnp.zeros((bp, h_m, 1, 1), F32)

    gate_pad = LANES - 2 * h_m
    rows = lambda a: a.reshape(depth, 1, -1).astype(F32)
    col = lambda a: a.reshape(depth, -1, 1).astype(F32)
    w_in_t = jnp.swapaxes(w_in, 1, 2)
    lw = {
        "w_main": w_in[:, :, :d_main].astype(BF16),
        "b_main": rows(b_in[:, :d_main]),
        "w_gate_in": jnp.pad(w_in[:, :, d_main:], ((0, 0), (0, 0), (0, gate_pad))).astype(BF16),
        "b_gate_in": rows(jnp.pad(b_in[:, d_main:] + mlstm_gate_b, ((0, 0), (0, gate_pad)))),
        "w_kvt": w_in_t[:, d_attn:3 * d_attn].astype(BF16),
        "b_kvt": col(b_in[:, d_attn:3 * d_attn]),
        "w_ml_t": w_in_t[:, d_main - 4 * d_ml:d_main].astype(BF16),
        "b_ml_t": col(b_in[:, d_main - 4 * d_ml:d_main]),
        "w_gate_t": w_in_t[:, d_main:].astype(BF16),
        "b_gate_t": col(b_in[:, d_main:] + mlstm_gate_b),
        "lam_re": rows(ssm_lambda_re),
        "lam_im": rows(ssm_lambda_im),
        "log_dt": rows(jnp.repeat(ssm_log_dt, n_p, axis=-1)),
        "braw_re": _block_diag(jnp.swapaxes(ssm_b_re, 2, 3)).astype(F32),
        "braw_im": _block_diag(jnp.swapaxes(ssm_b_im, 2, 3)).astype(F32),
        "c_re_bd": _block_diag(jnp.swapaxes(ssm_c_re, 2, 3)).astype(BF16),
        "c_im_bd": _block_diag(jnp.swapaxes(ssm_c_im, 2, 3)).astype(BF16),
        "d_skip": rows(ssm_d),
        "w_glu": ssm_w_glu.astype(BF16),
        "b_glu": rows(ssm_b_glu),
        "mix_g": rows(mix_g),
        "w_out": w_out.astype(BF16),
        "ln1_g": rows(ln1_g), "ln1_b": rows(ln1_b),
        "w_gate": w_gate.astype(BF16), "w_up": w_up.astype(BF16), "w_down": w_down.astype(BF16),
        "ln2_g": rows(ln2_g), "ln2_b": rows(ln2_b),
    }

    kv_all = (jnp.zeros((depth, bp * tp // page, d_attn, page), F32),) * 2
    outs_p = [[] for _ in range(5)]
    outs_s = [[] for _ in range(7)]
    for l in range(depth):
        proj = functools.partial(_proj, lw=lw, layer=l, d_attn=d_attn, d_ssm=d_ssm, d_ml=d_ml, q_scale=q_scale)

        q, *kv_all, kb, vb, us, qm, km, vm, om, gates = proj(xp, seq_len=tp, page=page, kv_all=kv_all)
        ya = _attn_prompt(q, kb, vb, bsz=bp, t=tp, hd=hd_a)
        ys, h_re, h_im = _s5(us.reshape(tp * bp, d_ssm), zero_state, zero_state, lw, l, t=tp)
        r3 = lambda a: a.reshape(bp, tp, -1)
        ym, c1, n1, m1 = _mlstm(r3(qm), r3(km), r3(vm), r3(om), r3(gates), zero_c, zero_n, zero_m, hd=hd_m)
        xp = _dense(xp, ya, ys.reshape(tp, bp * d_ssm), ym.reshape(bp * tp, d_ml), lw, l, alpha=alpha,
                    d_ssm=d_ssm, seq_len=tp)
        for lst, a in zip(outs_p, (h_re, h_im, c1, n1, m1)):
            lst.append(a)

        q, k, v, us = proj(xs)
        r3 = lambda a: a.reshape(bs, ts, -1)
        to_tm = lambda a: jnp.swapaxes(a.reshape(bs, ts, -1), 0, 1).reshape(ts * bs, -1)
        from_tm = lambda a: jnp.swapaxes(a.reshape(ts, bs, -1), 0, 1).reshape(bs * ts, -1)
        ya = _attn_sample(r3(q), r3(k), r3(v), cache_k2, cache_v2, pt_flat, l, n_heads=h_a, hd=hd_a)
        ys, h_re, h_im = _s5(to_tm(us), state_ssm_re[l].reshape(bs, n_state), state_ssm_im[l].reshape(bs, n_state),
                             lw, l, t=ts)
        x_t, g_t = _proj_t(to_tm(xs), lw, l, ts=ts)
        ym_t, c1, n1, m1 = _mlstm_seq(
            x_t.reshape(ts, 4, h_m, hd_m, bs),
            jnp.transpose(g_t.reshape(ts, 2, h_m, bs), (2, 0, 1, 3)),
            c_state_t, l, jnp.transpose(state_mlstm_n[l], (1, 2, 0)),
            state_mlstm_m[l].T.reshape(h_m, 1, bs), hd=hd_m)
        ym = jnp.transpose(ym_t.reshape(ts, d_ml, bs), (2, 0, 1)).reshape(bs * ts, d_ml)
        xs = _dense(xs, ya.reshape(bs * ts, d_attn), from_tm(ys), ym, lw, l, alpha=alpha, d_ssm=d_ssm)
        for lst, a in zip(outs_s, (k, v, h_re, h_im, c1, n1, m1)):
            lst.append(a)

    st = lambda lst: jnp.stack(lst)
    paged = lambda a: jnp.transpose(a.reshape(depth, bp, tp // page, h_a, hd_a, page), (0, 1, 2, 5, 3, 4))
    k_prompt, v_prompt = paged(kv_all[0]), paged(kv_all[1])
    k_sample = st(outs_s[0]).reshape(depth, bs, ts, h_a, hd_a)
    v_sample = st(outs_s[1]).reshape(depth, bs, ts, h_a, hd_a)
    ssm = lambda lst, b: st(lst).reshape(depth, b, n_groups, n_p)
    return (xp.reshape(bp, tp, d_model), xs.reshape(bs, ts, d_model),
            k_prompt, v_prompt, k_sample, v_sample,
            ssm(outs_p[0], bp), ssm(outs_p[1], bp), ssm(outs_s[2], bs), ssm(outs_s[3], bs),
            st(outs_p[2]), st(outs_p[3]).reshape(depth, bp, h_m, hd_m), st(outs_p[4]).reshape(depth, bp, h_m),
            jnp.transpose(st(outs_s[4]), (0, 4, 1, 2, 3)), jnp.transpose(st(outs_s[5]), (0, 3, 1, 2)),
            jnp.transpose(st(outs_s[6]).reshape(depth, h_m, bs), (0, 2, 1)))
```
